```python
import jax, jax.numpy as jnp
from jax import lax
import numpy as np

D_MODEL = 2048
BATCH = 1
SEQ = 8192
DEPTH = 1

HG_HEADS = 16
HG_KDIM = 128
HG_VDIM = D_MODEL // HG_HEADS
HG_FDIM = HG_HEADS * HG_KDIM
HG_VWIDTH = HG_HEADS * HG_VDIM
HG_CHUNK = 64
MLA_HEADS = 16
Q_LORA = 512
KV_LORA = 512
QK_NOPE = 128
QK_ROPE = 64
V_DIM = 128
QK_DIM = QK_NOPE + QK_ROPE
ROPE_THETA = 10000.0
ATTN_BLOCK = 128
FF_MULT = 256
D_FF = ((8 * D_MODEL + 3 * FF_MULT - 1) // (3 * FF_MULT)) * FF_MULT
N_MOD = 6
EPS = 1e-6
IN_SIZES = (HG_FDIM, HG_FDIM, HG_VWIDTH, HG_VWIDTH,
            Q_LORA, KV_LORA, QK_ROPE,
            D_MODEL, D_MODEL)
IN_WIDTH = sum(IN_SIZES)

kernel_name = "hybrid_hgrn2_mla_swiglu_sandwich_adaln"


def rmsnorm(x, w):
    xf = x.astype(jnp.float32)
    y = xf * lax.rsqrt(jnp.mean(xf * xf, axis=-1, keepdims=True) + EPS)
    return (y * w.astype(jnp.float32)).astype(x.dtype)


def rope_tables(positions):
    inv_freq = ROPE_THETA ** (-jnp.arange(0, QK_ROPE, 2, dtype=jnp.float32) / QK_ROPE)
    ang = positions.astype(jnp.float32)[..., None] * inv_freq
    return jnp.cos(ang), jnp.sin(ang)


def apply_rope(t, cos, sin):
    cos = cos.astype(t.dtype)
    sin = sin.astype(t.dtype)
    t1, t2 = jnp.split(t, 2, axis=-1)
    return jnp.concatenate([t1 * cos - t2 * sin, t2 * cos + t1 * sin], axis=-1)


def hgrn2_chunked(q, k, v, g):
    B, S, H, K = q.shape
    V = v.shape[-1]
    C = HG_CHUNK
    nc = S // C

    def to_chunks(t):
        return t.reshape(B, nc, C, H, t.shape[-1]).transpose(1, 0, 3, 2, 4)

    tri = jnp.tril(jnp.ones((C, C), dtype=bool))[:, :, None]

    def step(state, inp):
        qc, kc, vc, gc = inp
        b = jnp.cumsum(gc, axis=2)
        o_inter = jnp.einsum('bhck,bhkv->bhcv', qc * jnp.exp(b), state)
        diff = b[:, :, :, None, :] - b[:, :, None, :, :]
        decay = jnp.exp(jnp.where(tri, diff, -jnp.inf))
        scores = jnp.einsum('bhtsk,bhsk->bhts', qc[:, :, :, None, :] * decay, kc)
        o_intra = jnp.einsum('bhts,bhsv->bhtv', scores, vc)
        b_last = b[:, :, -1:, :]
        state = (jnp.exp(b_last[:, :, 0, :])[..., None] * state
                 + jnp.einsum('bhck,bhcv->bhkv', kc * jnp.exp(b_last - b), vc))
        return state, o_inter + o_intra

    s0 = jnp.zeros((B, H, K, V), jnp.float32)
    _, o = lax.scan(step, s0, (to_chunks(q), to_chunks(k), to_chunks(v), to_chunks(g)))
    return o.transpose(1, 0, 3, 2, 4).reshape(B, S, H, V)


def causal_block_attention(q, k, v):
    B, S, H, Dq = q.shape
    nb = S // ATTN_BLOCK
    qb = q.reshape(B, nb, ATTN_BLOCK, H, Dq).transpose(1, 0, 2, 3, 4)
    k_pos = jnp.arange(S)
    scale = QK_DIM ** -0.5

    def one_block(args):
        q_blk, blk = args
        s = jnp.einsum('bqhd,bkhd->bhqk', q_blk, k,
                       preferred_element_type=jnp.float32) * scale
        q_pos = blk * ATTN_BLOCK + jnp.arange(ATTN_BLOCK)
        s = jnp.where(k_pos[None, :] <= q_pos[:, None], s, -jnp.inf)
        p = jax.nn.softmax(s, axis=-1).astype(v.dtype)
        return jnp.einsum('bhqk,bkhd->bqhd', p, v)

    out = lax.map(one_block, (qb, jnp.arange(nb)))
    return out.transpose(1, 0, 2, 3, 4).reshape(B, S, H, v.shape[-1])


def token_mixer(h, cos, sin, lb, w_in, g_hg_out, w_o_hg, g_q_lora, w_uq,
                g_kv_lora, w_ukv, w_o_mla, w_out):
    B, S, _ = h.shape
    dt = h.dtype
    f32 = jnp.float32
    split_idx = tuple(int(i) for i in np.cumsum(IN_SIZES)[:-1])
    hq, hf, hi, hg, cq, ckv, kr, gate_a, gate_b = jnp.split(h @ w_in, split_idx, axis=-1)

    q = jax.nn.silu(hq.astype(f32)).reshape(B, S, HG_HEADS, HG_KDIM)
    f = lb + (1.0 - lb) * jax.nn.sigmoid(hf.astype(f32))
    log_f = jnp.log(f).reshape(B, S, HG_HEADS, HG_KDIM)
    k = (1.0 - f).reshape(B, S, HG_HEADS, HG_KDIM)
    v = hi.astype(f32).reshape(B, S, HG_HEADS, HG_VDIM)
    o = hgrn2_chunked(q, k, v, log_f)
    o = rmsnorm(o, g_hg_out) * jax.nn.silu(hg.astype(f32)).reshape(B, S, HG_HEADS, HG_VDIM)
    y_a = o.reshape(B, S, HG_VWIDTH).astype(dt) @ w_o_hg

    q_m = (rmsnorm(cq, g_q_lora) @ w_uq).reshape(B, S, MLA_HEADS, QK_DIM)
    q_nope, q_rope = jnp.split(q_m, [QK_NOPE], axis=-1)
    kv = (rmsnorm(ckv, g_kv_lora) @ w_ukv).reshape(B, S, MLA_HEADS, QK_NOPE + V_DIM)
    k_nope, v_m = jnp.split(kv, [QK_NOPE], axis=-1)
    q_rope = apply_rope(q_rope, cos[:, :, None, :], sin[:, :, None, :])
    k_rope = apply_rope(kr, cos, sin)
    q_full = jnp.concatenate([q_nope, q_rope], axis=-1)
    k_full = jnp.concatenate(
        [k_nope, jnp.broadcast_to(k_rope[:, :, None, :], (B, S, MLA_HEADS, QK_ROPE))], axis=-1)
    att = causal_block_attention(q_full, k_full, v_m)
    y_b = att.reshape(B, S, MLA_HEADS * V_DIM) @ w_o_mla

    merged = jax.nn.sigmoid(gate_a) * y_a + jax.nn.sigmoid(gate_b) * y_b
    return merged @ w_out


def swiglu(h, w_gate_up, w_down):
    gate, up = jnp.split(h @ w_gate_up, 2, axis=-1)
    return (jax.nn.silu(gate) * up) @ w_down


def setup_inputs(seed: int = 0) -> dict:
    key = jax.random.key(seed)
    ks = jax.random.split(key, 24)
    f32 = jnp.float32

    def nrm(k, shape, scale):
        return jax.random.normal(k, shape, f32) * scale

    def gain(k, shape):
        return 1.0 + 0.02 * jax.random.normal(k, shape, f32)

    L = DEPTH
    x = jax.random.normal(ks[0], (BATCH, SEQ, D_MODEL), f32)
    c = jax.random.normal(ks[1], (BATCH, D_MODEL), f32)
    offset = jax.random.randint(ks[2], (BATCH, 1), 0, 4096, dtype=jnp.int32)
    positions = (jnp.arange(SEQ, dtype=jnp.int32)[None, :] + offset).astype(jnp.int32)
    return {
        "x": x,
        "c": c,
        "positions": positions,
        "w_ada": nrm(ks[3], (L, D_MODEL, N_MOD * D_MODEL), 0.5 * D_MODEL ** -0.5),
        "b_ada": nrm(ks[4], (L, N_MOD * D_MODEL), 0.01),
        "g_pre_mix": gain(ks[5], (L, D_MODEL)),
        "w_in": nrm(ks[6], (L, D_MODEL, IN_WIDTH), D_MODEL ** -0.5),
        "lb_logits": nrm(ks[7], (DEPTH + 1, HG_FDIM), 0.1),
        "g_hg_out": gain(ks[8], (L, HG_VDIM)),
        "w_o_hg": nrm(ks[9], (L, HG_VWIDTH, D_MODEL), HG_VWIDTH ** -0.5),
        "g_q_lora": gain(ks[10], (L, Q_LORA)),
        "w_uq": nrm(ks[11], (L, Q_LORA, MLA_HEADS * QK_DIM), Q_LORA ** -0.5),
        "g_kv_lora": gain(ks[12], (L, KV_LORA)),
        "w_ukv": nrm(ks[13], (L, KV_LORA, MLA_HEADS * (QK_NOPE + V_DIM)), KV_LORA ** -0.5),
        "w_o_mla": nrm(ks[14], (L, MLA_HEADS * V_DIM, D_MODEL), (MLA_HEADS * V_DIM) ** -0.5),
        "w_out": nrm(ks[15], (L, D_MODEL, D_MODEL), D_MODEL ** -0.5),
        "g_post_mix": gain(ks[16], (L, D_MODEL)),
        "g_pre_ffn": gain(ks[17], (L, D_MODEL)),
        "w_gate_up": nrm(ks[18], (L, D_MODEL, 2 * D_FF), D_MODEL ** -0.5),
        "w_down": nrm(ks[19], (L, D_FF, D_MODEL), D_FF ** -0.5),
        "g_post_ffn": gain(ks[20], (L, D_MODEL)),
    }


def reference(x, c, positions, w_ada, b_ada, g_pre_mix, w_in, lb_logits, g_hg_out,
              w_o_hg, g_q_lora, w_uq, g_kv_lora, w_ukv, w_o_mla, w_out, g_post_mix,
              g_pre_ffn, w_gate_up, w_down, g_post_ffn):
    lb_all = jnp.cumsum(jax.nn.softmax(lb_logits.astype(jnp.float32), axis=0), axis=0)
    cos, sin = rope_tables(positions)
    for l in range(DEPTH):
        mod = jax.nn.silu(c) @ w_ada[l] + b_ada[l]
        sh_m, sc_m, ga_m, sh_f, sc_f, ga_f = jnp.split(mod[:, None, :], N_MOD, axis=-1)
        h = rmsnorm(x, g_pre_mix[l]) * (1.0 + sc_m) + sh_m
        y = token_mixer(h, cos, sin, lb_all[l], w_in[l], g_hg_out[l], w_o_hg[l],
                        g_q_lora[l], w_uq[l], g_kv_lora[l], w_ukv[l], w_o_mla[l], w_out[l])
        x = x + ga_m * rmsnorm(y, g_post_mix[l])
        h = rmsnorm(x, g_pre_ffn[l]) * (1.0 + sc_f) + sh_f
        y = swiglu(h, w_gate_up[l], w_down[l])
        x = x + ga_f * rmsnorm(y, g_post_ffn[l])
    return x
```

```python
import functools

import jax
import jax.numpy as jnp
from jax import lax
from jax.experimental import pallas as pl
from jax.experimental.pallas import tpu as pltpu

F32 = jnp.float32
BF16 = jnp.bfloat16

V7X_VMEM_BYTES = 64 * 1024 * 1024
V7X_LANES = 128
V7X_SUBLANES = 8

EPS = 1e-6
N_HEADS = 16
HEAD_DIM = 128
ROPE_DIM = 64
ROPE_HALF = ROPE_DIM // 2
QK_DIM = HEAD_DIM + ROPE_DIM
QK_PAD = 2 * HEAD_DIM
LORA = 512
ROPE_THETA = 10000.0
NEG = -1e30

HGRN_CHUNK = 128
HGRN_ROWS = 1024
ATTN_TILE = 512


def _vmem_limit(*nbytes):
    need = int(sum(nbytes))
    return int(min(V7X_VMEM_BYTES - (4 << 20), need + (8 << 20)))


def _nbytes(shape, dtype):
    n = 1
    for s in shape:
        n *= s
    return n * jnp.dtype(dtype).itemsize


def _row_spec(arr_shape, tm):
    if len(arr_shape) == 2:
        return pl.BlockSpec((tm, arr_shape[1]), lambda i: (i, 0))
    return pl.BlockSpec((arr_shape[0], tm, arr_shape[2]), lambda i: (0, i, 0))


def _const_spec(arr_shape):
    nd = len(arr_shape)
    return pl.BlockSpec(tuple(arr_shape), lambda i: (0,) * nd)


def _row_call(body, row_ins, const_ins, outs, tm, temp_bytes, name):
    m = row_ins[0].shape[-2]
    assert m % tm == 0, (m, tm)
    in_specs = [_row_spec(a.shape, tm) for a in row_ins] + [_const_spec(a.shape) for a in const_ins]
    out_specs = [_row_spec(o.shape, tm) for o in outs]
    tile = lambda s: tuple(s[:-2]) + (tm, s[-1])
    need = (2 * sum(_nbytes(tile(a.shape), a.dtype) for a in row_ins)
            + 2 * sum(_nbytes(tile(o.shape), o.dtype) for o in outs)
            + sum(_nbytes(a.shape, a.dtype) for a in const_ins))
    res = pl.pallas_call(
        body,
        grid=(m // tm,),
        in_specs=in_specs,
        out_specs=out_specs,
        out_shape=outs,
        compiler_params=pltpu.CompilerParams(
            dimension_semantics=("arbitrary",),
            vmem_limit_bytes=_vmem_limit(need, temp_bytes)),
        name=name,
    )(*row_ins, *const_ins)
    return res


def _sds(shape, dtype):
    return jax.ShapeDtypeStruct(tuple(shape), dtype)


def _rms(y, gain):
    return y * lax.rsqrt(jnp.mean(y * y, axis=-1, keepdims=True) + EPS) * gain


def _rope128(x, c, a, b):
    return x * c + pltpu.roll(x, 96, 1) * a + pltpu.roll(x, 32, 1) * b


def _ada_body(c_ref, w_ref, b_ref, o_ref):
    cc = c_ref[...]
    sc = cc * jax.nn.sigmoid(cc)
    o_ref[...] = jnp.sum(sc * w_ref[...], axis=0, keepdims=True) + b_ref[...]


def _ada(c_col, w, b_row):
    d, n = w.shape
    tn = 1024
    return pl.pallas_call(
        _ada_body,
        grid=(n // tn,),
        in_specs=[pl.BlockSpec((d, 1), lambda j: (0, 0)),
                  pl.BlockSpec((d, tn), lambda j: (0, j)),
                  pl.BlockSpec((1, tn), lambda j: (0, j))],
        out_specs=pl.BlockSpec((1, tn), lambda j: (0, j)),
        out_shape=_sds((1, n), F32),
        compiler_params=pltpu.CompilerParams(
            dimension_semantics=("arbitrary",),
            vmem_limit_bytes=_vmem_limit(3 * _nbytes((d, tn), F32), _nbytes((d, V7X_LANES), F32))),
        name="ada",
    )(c_col, w, b_row)


def _ropetab_body(pos_ref, inv_ref, c_ref, a_ref, b_ref):
    ang = pos_ref[...].astype(F32) * inv_ref[...]
    cos = jnp.cos(ang)
    sin = jnp.sin(ang)
    lane = lax.broadcasted_iota(jnp.int32, ang.shape, 1)
    c_ref[...] = jnp.where(lane < ROPE_DIM, cos, 0.0)
    a_ref[...] = jnp.where(lane < ROPE_HALF, -sin, 0.0)
    b_ref[...] = jnp.where((lane >= ROPE_HALF) & (lane < ROPE_DIM), sin, 0.0)


def _prenorm_body(x_ref, g_ref, sc_ref, sh_ref, h_ref):
    x = x_ref[...]
    h_ref[...] = (_rms(x, g_ref[...]) * (1.0 + sc_ref[...]) + sh_ref[...]).astype(BF16)


def _heads_out(o_ref, c0, val):
    for hh in range(val.shape[1] // HEAD_DIM):
        o_ref[c0 // HEAD_DIM + hh] = val[:, hh * HEAD_DIM:(hh + 1) * HEAD_DIM]


def _mm_chunks(a, w_ref, nc):
    n = w_ref.shape[1]
    for c0 in range(0, n, nc):
        yield c0, jnp.dot(a, w_ref[:, c0:c0 + nc], preferred_element_type=F32)


def _in_silu_body(h_ref, w_ref, o_ref):
    for c0, r in _mm_chunks(h_ref[...], w_ref, 512):
        _heads_out(o_ref, c0, (r * jax.nn.sigmoid(r)).astype(BF16))


def _in_plain_body(h_ref, w_ref, o_ref):
    for c0, r in _mm_chunks(h_ref[...], w_ref, 512):
        _heads_out(o_ref, c0, r.astype(BF16))


def _in_forget_body(h_ref, w_ref, lb_ref, g_ref, k_ref):
    logits = lb_ref[...]
    e = jnp.exp(logits - jnp.max(logits, axis=0, keepdims=True))
    lb_all = e[0:1, :] / jnp.sum(e, axis=0, keepdims=True)
    for c0, r in _mm_chunks(h_ref[...], w_ref, 512):
        lb = lb_all[:, c0:c0 + 512]
        f = lb + (1.0 - lb) * jax.nn.sigmoid(r)
        _heads_out(g_ref, c0, jnp.log(f))
        _heads_out(k_ref, c0, (1.0 - f).astype(BF16))


def _in_lora_body(h_ref, w_ref, gain_ref, o_ref):
    r = jnp.dot(h_ref[...], w_ref[...], preferred_element_type=F32)
    o_ref[...] = _rms(r, gain_ref[...]).astype(BF16)


def _in_krope_body(h_ref, c_ref, a_ref, b_ref, w_ref, o_ref):
    r = jnp.dot(h_ref[...], w_ref[...], preferred_element_type=F32)
    o_ref[...] = _rope128(r, c_ref[...], a_ref[...], b_ref[...]).astype(BF16)


def _in_gate_body(h_ref, w_ref, o_ref):
    for c0, r in _mm_chunks(h_ref[...], w_ref, 512):
        o_ref[:, c0:c0 + 512] = jax.nn.sigmoid(r).astype(BF16)


def _qup_body(cq_ref, c_ref, a_ref, b_ref, w_ref, q_ref):
    scale = QK_DIM ** -0.5
    cq = cq_ref[...]
    c, a, b = c_ref[...], a_ref[...], b_ref[...]
    for c0, r in _mm_chunks(cq, w_ref, 2 * QK_PAD):
        for hh in range(2):
            head = c0 // QK_PAD + hh
            nope = r[:, hh * QK_PAD:hh * QK_PAD + HEAD_DIM]
            rope = _rope128(r[:, hh * QK_PAD + HEAD_DIM:(hh + 1) * QK_PAD], c, a, b)
            q_ref[head, :, 0:HEAD_DIM] = (nope * scale).astype(BF16)
            q_ref[head, :, HEAD_DIM:QK_PAD] = (rope * scale).astype(BF16)


def _kvup_body(ckv_ref, kr_ref, w_ref, k_ref, v_ref):
    ckv = ckv_ref[...]
    kr = kr_ref[...]
    for c0, r in _mm_chunks(ckv, w_ref, 2 * QK_PAD):
        for hh in range(2):
            head = c0 // QK_PAD + hh
            k_ref[head, :, 0:HEAD_DIM] = r[:, hh * QK_PAD:hh * QK_PAD + HEAD_DIM].astype(BF16)
            k_ref[head, :, HEAD_DIM:QK_PAD] = kr
            v_ref[head] = r[:, hh * QK_PAD + HEAD_DIM:(hh + 1) * QK_PAD].astype(BF16)


def _attn_body(q_ref, k_ref, v_ref, o_ref, *, t):
    qi = pl.program_id(1)
    q = q_ref[0]

    def block(j, carry, masked):
        m, l, acc = carry
        r0 = pl.multiple_of(j * t, t)
        kb = k_ref[0, pl.ds(r0, t), :]
        vb = v_ref[0, pl.ds(r0, t), :]
        s = lax.dot_general(q, kb, (((1,), (1,)), ((), ())), preferred_element_type=F32)
        if masked:
            row = lax.broadcasted_iota(jnp.int32, (t, t), 0)
            col = lax.broadcasted_iota(jnp.int32, (t, t), 1)
            s = jnp.where(col <= row, s, NEG)
        m_new = jnp.maximum(m, jnp.max(s, axis=-1, keepdims=True))
        alpha = jnp.exp(m - m_new)
        p = jnp.exp(s - m_new)
        l = alpha * l + jnp.sum(p, axis=-1, keepdims=True)
        acc = alpha * acc + jnp.dot(p.astype(BF16), vb, preferred_element_type=F32)
        return m_new, l, acc

    init = (jnp.full((t, 1), NEG, F32), jnp.zeros((t, 1), F32), jnp.zeros((t, HEAD_DIM), F32))
    carry = lax.fori_loop(0, qi, lambda j, cr: block(j, cr, False), init)
    _, l, acc = block(qi, carry, True)
    o_ref[0] = (acc / l).astype(BF16)


def _attention(q, k, v):
    h, s, _ = q.shape
    t = min(ATTN_TILE, s)
    kv_bytes = _nbytes((s, QK_PAD), BF16) + _nbytes((s, HEAD_DIM), BF16)
    return pl.pallas_call(
        functools.partial(_attn_body, t=t),
        grid=(h, s // t),
        in_specs=[pl.BlockSpec((1, t, QK_PAD), lambda hh, i: (hh, i, 0)),
                  pl.BlockSpec((1, s, QK_PAD), lambda hh, i: (hh, 0, 0)),
                  pl.BlockSpec((1, s, HEAD_DIM), lambda hh, i: (hh, 0, 0))],
        out_specs=pl.BlockSpec((1, t, HEAD_DIM), lambda hh, i: (hh, i, 0)),
        out_shape=_sds((h, s, HEAD_DIM), BF16),
        compiler_params=pltpu.CompilerParams(
            dimension_semantics=("arbitrary", "arbitrary"),
            vmem_limit_bytes=_vmem_limit(2 * kv_bytes, 4 * _nbytes((t, QK_PAD), BF16),
                                         6 * _nbytes((t, t), F32))),
        name="attn",
    )(q, k, v)


def _hgrn_chunk(qf, kf, vf, g, st, tril, ones, c):
    nt = (((1,), (1,)), ((), ()))
    b = jnp.dot(tril, g, preferred_element_type=F32, precision=lax.Precision.HIGHEST)
    vb = vf.astype(BF16)

    o = lax.dot_general((qf * jnp.exp(b)).astype(BF16), st.astype(BF16), nt, preferred_element_type=F32)

    row = lax.broadcasted_iota(jnp.int32, (c, c), 0)
    col = lax.broadcasted_iota(jnp.int32, (c, c), 1)
    scores = None
    m = c // 2
    while m >= V7X_SUBLANES:
        n = c // (2 * m)
        b3 = b.reshape(n, 2 * m, HEAD_DIM)
        mid = b3[:, m - 1:m, :]
        second = lax.broadcasted_iota(jnp.int32, b3.shape, 1) >= m
        qt = jnp.exp(jnp.where(second, b3 - mid, NEG)) * qf.reshape(b3.shape)
        kt = jnp.exp(jnp.where(second, NEG, mid - b3)) * kf.reshape(b3.shape)
        s_l = lax.dot_general(qt.reshape(c, HEAD_DIM).astype(BF16), kt.reshape(c, HEAD_DIM).astype(BF16),
                              nt, preferred_element_type=F32)
        if scores is None:
            scores = s_l
        else:
            shift = (2 * m).bit_length() - 1
            scores = jnp.where((row >> shift) == (col >> shift), s_l, scores)
        m //= 2
    o = o + jnp.dot(scores.astype(BF16), vb, preferred_element_type=F32)

    n8 = c // V7X_SUBLANES
    shp = (n8, V7X_SUBLANES, HEAD_DIM)
    b8, q8, k8, v8 = b.reshape(shp), qf.reshape(shp), kf.reshape(shp), vf.reshape(shp)
    pos = lax.broadcasted_iota(jnp.int32, shp, 1)
    for s in range(V7X_SUBLANES):
        arg = b8 - b8[:, s:s + 1, :]
        if s:
            arg = jnp.where(pos >= s, arg, NEG)
        a = jnp.exp(arg) * q8 * k8[:, s:s + 1, :]
        rs = jnp.dot(a.reshape(c, HEAD_DIM).astype(BF16), ones, preferred_element_type=F32)
        o = o + rs * jnp.broadcast_to(v8[:, s:s + 1, :], shp).reshape(c, HEAD_DIM)

    b_last = b[c - 1:c, :]
    kd = (kf * jnp.exp(b_last - b)).astype(BF16)
    st_new = st * jnp.exp(b_last) + lax.dot_general(vb, kd, (((0,), (0,)), ((), ())),
                                                    preferred_element_type=F32)
    return o, st_new


def _hgrn_body(q_ref, g_ref, k_ref, v_ref, og_ref, gain_ref, o_ref, st_ref, *, c, rows):
    @pl.when(pl.program_id(1) == 0)
    def _():
        st_ref[...] = jnp.zeros_like(st_ref)

    tril = (lax.broadcasted_iota(jnp.int32, (c, c), 0) >= lax.broadcasted_iota(jnp.int32, (c, c), 1)).astype(F32)
    ones = jnp.ones((HEAD_DIM, HEAD_DIM), BF16)
    gain = gain_ref[...]

    def step(i, carry):
        r0 = pl.multiple_of(i * c, c)
        sl = pl.ds(r0, c)
        o, st_new = _hgrn_chunk(q_ref[0, sl, :].astype(F32), k_ref[0, sl, :].astype(F32),
                                v_ref[0, sl, :].astype(F32), g_ref[0, sl, :], st_ref[...], tril, ones, c)
        st_ref[...] = st_new
        o_ref[0, sl, :] = (_rms(o, gain) * og_ref[0, sl, :].astype(F32)).astype(BF16)
        return carry

    lax.fori_loop(0, rows // c, step, 0)


def _hgrn(q, g, k, v, og, gain):
    h, s, d = q.shape
    rows = min(HGRN_ROWS, s)
    c = min(HGRN_CHUNK, rows)
    spec = pl.BlockSpec((1, rows, d), lambda hh, i: (hh, i, 0))
    return pl.pallas_call(
        functools.partial(_hgrn_body, c=c, rows=rows),
        grid=(h, s // rows),
        in_specs=[spec, spec, spec, spec, spec, pl.BlockSpec((1, d), lambda hh, i: (0, 0))],
        out_specs=spec,
        out_shape=_sds((h, s, d), BF16),
        scratch_shapes=[pltpu.VMEM((d, d), F32)],
        compiler_params=pltpu.CompilerParams(
            dimension_semantics=("arbitrary", "arbitrary"),
            vmem_limit_bytes=_vmem_limit(2 * 5 * _nbytes((rows, d), F32), 64 * _nbytes((c, d), F32))),
        name="hgrn",
    )(q, g, k, v, og, gain)


def _cat_heads(ref):
    return jnp.concatenate([ref[hh] for hh in range(ref.shape[0])], axis=-1)


def _mix_body(oa_ref, ob_ref, sa_ref, sb_ref, x_ref,
              woa_ref, wob_ref, wout_ref, gpost_ref, gate_ref, gpre_ref, sc_ref, sh_ref,
              x1_ref, h2_ref):
    ya = jnp.dot(_cat_heads(oa_ref), woa_ref[...], preferred_element_type=F32)
    yb = jnp.dot(_cat_heads(ob_ref), wob_ref[...], preferred_element_type=F32)
    merged = sa_ref[...].astype(F32) * ya + sb_ref[...].astype(F32) * yb
    y = jnp.dot(merged.astype(BF16), wout_ref[...], preferred_element_type=F32)
    x1 = x_ref[...] + gate_ref[...] * _rms(y, gpost_ref[...])
    x1_ref[...] = x1
    h2_ref[...] = (_rms(x1, gpre_ref[...]) * (1.0 + sc_ref[...]) + sh_ref[...]).astype(BF16)


def _gateup_body(h_ref, wg_ref, wu_ref, o_ref):
    h = h_ref[...]
    gte = jnp.dot(h, wg_ref[...], preferred_element_type=F32)
    up = jnp.dot(h, wu_ref[...], preferred_element_type=F32)
    o_ref[...] = (gte * jax.nn.sigmoid(gte) * up).astype(BF16)


def _gateup(h2, w_gu, d_ff):
    s, d = h2.shape
    tm = min(1024, s)
    tn = 512
    nj = d_ff // tn
    return pl.pallas_call(
        _gateup_body,
        grid=(s // tm, nj),
        in_specs=[pl.BlockSpec((tm, d), lambda i, j: (i, 0)),
                  pl.BlockSpec((d, tn), lambda i, j: (0, j)),
                  pl.BlockSpec((d, tn), lambda i, j: (0, j + nj))],
        out_specs=pl.BlockSpec((tm, tn), lambda i, j: (i, j)),
        out_shape=_sds((s, d_ff), BF16),
        compiler_params=pltpu.CompilerParams(
            dimension_semantics=("arbitrary", "arbitrary"),
            vmem_limit_bytes=_vmem_limit(2 * _nbytes((tm, d), BF16), 4 * _nbytes((d, tn), BF16),
                                         2 * _nbytes((tm, tn), BF16), 4 * _nbytes((tm, tn), F32))),
        name="gateup",
    )(h2, w_gu, w_gu)


def _down_body(a_ref, x_ref, w_ref, gpost_ref, gate_ref, o_ref):
    y = jnp.dot(a_ref[...], w_ref[...], preferred_element_type=F32)
    o_ref[...] = x_ref[...] + gate_ref[...] * _rms(y, gpost_ref[...])


def kernel(x, c, positions, w_ada, b_ada, g_pre_mix, w_in, lb_logits, g_hg_out, w_o_hg, g_q_lora, w_uq,
           g_kv_lora, w_ukv, w_o_mla, w_out, g_post_mix, g_pre_ffn, w_gate_up, w_down, g_post_ffn):
    bsz, s, d = x.shape
    assert bsz == 1 and d == N_HEADS * HEAD_DIM
    d_ff = w_down.shape[1]
    x2 = x.reshape(s, d)
    row = lambda v: v.reshape(1, -1)
    hm = lambda dt: _sds((N_HEADS, s, HEAD_DIM), dt)

    wi = w_in[0]
    o_q, o_f, o_i, o_g, o_cq, o_ckv, o_kr, o_ga, o_gb = (
        0, d, 2 * d, 3 * d, 4 * d, 4 * d + LORA, 4 * d + 2 * LORA, 4 * d + 2 * LORA + ROPE_DIM,
        5 * d + 2 * LORA + ROPE_DIM)
    w_hq = wi[:, o_q:o_f].astype(BF16)
    w_hf = wi[:, o_f:o_i].astype(BF16)
    w_hi = wi[:, o_i:o_g].astype(BF16)
    w_hg = wi[:, o_g:o_cq].astype(BF16)
    w_cq = wi[:, o_cq:o_ckv].astype(BF16)
    w_ckv = wi[:, o_ckv:o_kr].astype(BF16)
    w_kr = jnp.pad(wi[:, o_kr:o_ga], ((0, 0), (0, HEAD_DIM - ROPE_DIM))).astype(BF16)
    w_ga = wi[:, o_ga:o_gb].astype(BF16)
    w_gb = wi[:, o_gb:].astype(BF16)
    w_uq_p = jnp.pad(w_uq[0].reshape(LORA, N_HEADS, QK_DIM),
                     ((0, 0), (0, 0), (0, QK_PAD - QK_DIM))).reshape(LORA, N_HEADS * QK_PAD).astype(BF16)
    w_ukv_b = w_ukv[0].astype(BF16)
    inv_freq = ROPE_THETA ** (-jnp.arange(0, ROPE_DIM, 2, dtype=F32) / ROPE_DIM)
    inv_row = jnp.tile(inv_freq, V7X_LANES // ROPE_HALF).reshape(1, V7X_LANES)

    mod = _ada(c.reshape(d, 1), w_ada[0], row(b_ada[0]))
    sh_m, sc_m, ga_m, sh_f, sc_f, ga_f = (mod[:, i * d:(i + 1) * d] for i in range(6))

    tm = min(1024, s)
    tab = _sds((s, V7X_LANES), F32)
    rc, ra, rb = _row_call(_ropetab_body, [positions.reshape(s, 1)], [inv_row], [tab, tab, tab],
                           tm, 8 * _nbytes((tm, V7X_LANES), F32), "ropetab")

    tm = min(512, s)
    (h,) = _row_call(_prenorm_body, [x2], [row(g_pre_mix[0]), sc_m, sh_m], [_sds((s, d), BF16)],
                     tm, 3 * _nbytes((tm, d), F32), "prenorm")

    tmp = 6 * _nbytes((tm, 512), F32)
    (q_h,) = _row_call(_in_silu_body, [h], [w_hq], [hm(BF16)], tm, tmp, "in_q")
    g_h, k_h = _row_call(_in_forget_body, [h], [w_hf, lb_logits], [hm(F32), hm(BF16)], tm, tmp, "in_f")
    (v_h,) = _row_call(_in_plain_body, [h], [w_hi], [hm(BF16)], tm, tmp, "in_v")
    (og_h,) = _row_call(_in_silu_body, [h], [w_hg], [hm(BF16)], tm, tmp, "in_og")
    (cq_n,) = _row_call(_in_lora_body, [h], [w_cq, row(g_q_lora[0])], [_sds((s, LORA), BF16)], tm, tmp, "in_cq")
    (ckv_n,) = _row_call(_in_lora_body, [h], [w_ckv, row(g_kv_lora[0])], [_sds((s, LORA), BF16)], tm, tmp,
                         "in_ckv")
    (k_rope,) = _row_call(_in_krope_body, [h, rc, ra, rb], [w_kr], [_sds((s, V7X_LANES), BF16)], tm, tmp,
                          "in_kr")
    (sig_a,) = _row_call(_in_gate_body, [h], [w_ga], [_sds((s, d), BF16)], tm, tmp, "in_ga")
    (sig_b,) = _row_call(_in_gate_body, [h], [w_gb], [_sds((s, d), BF16)], tm, tmp, "in_gb")

    o_a = _hgrn(q_h, g_h, k_h, v_h, og_h, row(g_hg_out[0]))

    (q_m,) = _row_call(_qup_body, [cq_n, rc, ra, rb], [w_uq_p], [_sds((N_HEADS, s, QK_PAD), BF16)],
                       tm, tmp, "qup")
    k_m, v_m = _row_call(_kvup_body, [ckv_n, k_rope], [w_ukv_b],
                         [_sds((N_HEADS, s, QK_PAD), BF16), hm(BF16)], tm, tmp, "kvup")
    o_b = _attention(q_m, k_m, v_m)

    tm = min(256, s)
    x1, h2 = _row_call(
        _mix_body, [o_a, o_b, sig_a, sig_b, x2],
        [w_o_hg[0].astype(BF16), w_o_mla[0].astype(BF16), w_out[0].astype(BF16),
         row(g_post_mix[0]), ga_m, row(g_pre_ffn[0]), sc_f, sh_f],
        [_sds((s, d), F32), _sds((s, d), BF16)], tm, 8 * _nbytes((tm, d), F32), "mix")

    act = _gateup(h2, w_gate_up[0].astype(BF16), d_ff)
    (out,) = _row_call(_down_body, [act, x1], [w_down[0].astype(BF16), row(g_post_ffn[0]), ga_f],
                       [_sds((s, d), F32)], tm, 4 * _nbytes((tm, d), F32), "down")
    return out.reshape(bsz, s, d)
```

```python
import functools

import jax
import jax.numpy as jnp
from jax import lax
from jax.experimental import pallas as pl
from jax.experimental.pallas import tpu as pltpu

F32 = jnp.float32
BF16 = jnp.bfloat16

V7X_VMEM_BYTES = 64 * 1024 * 1024
V7X_LANES = 128
V7X_SUBLANES = 8

EPS = 1e-6
N_HEADS = 16
HEAD_DIM = 128
ROPE_DIM = 64
ROPE_HALF = ROPE_DIM // 2
QK_DIM = HEAD_DIM + ROPE_DIM
QK_PAD = 2 * HEAD_DIM
LORA = 512
ROPE_THETA = 10000.0
NEG = -1e30

HGRN_CHUNK = 128
HGRN_ROWS = 1024
HGRN_HEADS = 2
ATTN_TILE = 512
ATTN_KV_TILE = 1024
ATTN_HEADS = 2


def _vmem_limit(*nbytes):
    need = int(sum(nbytes))
    return int(min(V7X_VMEM_BYTES - (4 << 20), need + (8 << 20)))


def _nbytes(shape, dtype):
    n = 1
    for s in shape:
        n *= s
    return n * jnp.dtype(dtype).itemsize


def _row_spec(arr_shape, tm):
    if len(arr_shape) == 2:
        return pl.BlockSpec((tm, arr_shape[1]), lambda i: (i, 0))
    return pl.BlockSpec((arr_shape[0], tm, arr_shape[2]), lambda i: (0, i, 0))


def _const_spec(arr_shape):
    nd = len(arr_shape)
    return pl.BlockSpec(tuple(arr_shape), lambda i: (0,) * nd)


def _row_call(body, row_ins, const_ins, outs, tm, temp_bytes, name):
    m = row_ins[0].shape[-2]
    assert m % tm == 0, (m, tm)
    in_specs = [_row_spec(a.shape, tm) for a in row_ins] + [_const_spec(a.shape) for a in const_ins]
    out_specs = [_row_spec(o.shape, tm) for o in outs]
    tile = lambda s: tuple(s[:-2]) + (tm, s[-1])
    need = (2 * sum(_nbytes(tile(a.shape), a.dtype) for a in row_ins)
            + 2 * sum(_nbytes(tile(o.shape), o.dtype) for o in outs)
            + sum(_nbytes(a.shape, a.dtype) for a in const_ins))
    res = pl.pallas_call(
        body,
        grid=(m // tm,),
        in_specs=in_specs,
        out_specs=out_specs,
        out_shape=outs,
        compiler_params=pltpu.CompilerParams(
            dimension_semantics=("arbitrary",),
            vmem_limit_bytes=_vmem_limit(need, temp_bytes)),
        name=name,
    )(*row_ins, *const_ins)
    return res


def _sds(shape, dtype):
    return jax.ShapeDtypeStruct(tuple(shape), dtype)


def _rms(y, gain):
    return y * lax.rsqrt(jnp.mean(y * y, axis=-1, keepdims=True) + EPS) * gain


def _rope128(x, c, a, b):
    return x * c + pltpu.roll(x, 96, 1) * a + pltpu.roll(x, 32, 1) * b


def _ada_body(c_ref, w_ref, b_ref, o_ref):
    cc = c_ref[...]
    sc = cc * jax.nn.sigmoid(cc)
    o_ref[...] = jnp.sum(sc * w_ref[...], axis=0, keepdims=True) + b_ref[...]


def _ada(c_col, w, b_row):
    d, n = w.shape
    tn = 1024
    return pl.pallas_call(
        _ada_body,
        grid=(n // tn,),
        in_specs=[pl.BlockSpec((d, 1), lambda j: (0, 0)),
                  pl.BlockSpec((d, tn), lambda j: (0, j)),
                  pl.BlockSpec((1, tn), lambda j: (0, j))],
        out_specs=pl.BlockSpec((1, tn), lambda j: (0, j)),
        out_shape=_sds((1, n), F32),
        compiler_params=pltpu.CompilerParams(
            dimension_semantics=("arbitrary",),
            vmem_limit_bytes=_vmem_limit(3 * _nbytes((d, tn), F32), _nbytes((d, V7X_LANES), F32))),
        name="ada",
    )(c_col, w, b_row)


def _ropetab_body(pos_ref, inv_ref, c_ref, a_ref, b_ref):
    ang = pos_ref[...].astype(F32) * inv_ref[...]
    cos = jnp.cos(ang)
    sin = jnp.sin(ang)
    lane = lax.broadcasted_iota(jnp.int32, ang.shape, 1)
    c_ref[...] = jnp.where(lane < ROPE_DIM, cos, 0.0)
    a_ref[...] = jnp.where(lane < ROPE_HALF, -sin, 0.0)
    b_ref[...] = jnp.where((lane >= ROPE_HALF) & (lane < ROPE_DIM), sin, 0.0)


def _prenorm_body(x_ref, g_ref, sc_ref, sh_ref, h_ref):
    x = x_ref[...]
    h_ref[...] = (_rms(x, g_ref[...]) * (1.0 + sc_ref[...]) + sh_ref[...]).astype(BF16)


def _heads_out(o_ref, c0, val):
    for hh in range(val.shape[1] // HEAD_DIM):
        o_ref[c0 // HEAD_DIM + hh] = val[:, hh * HEAD_DIM:(hh + 1) * HEAD_DIM]


def _mm_chunks(a, w_ref, nc):
    n = w_ref.shape[1]
    for c0 in range(0, n, nc):
        yield c0, jnp.dot(a, w_ref[:, c0:c0 + nc], preferred_element_type=F32)


def _in_silu_body(h_ref, w_ref, o_ref):
    for c0, r in _mm_chunks(h_ref[...], w_ref, 512):
        _heads_out(o_ref, c0, (r * jax.nn.sigmoid(r)).astype(BF16))


def _in_plain_body(h_ref, w_ref, o_ref):
    for c0, r in _mm_chunks(h_ref[...], w_ref, 512):
        _heads_out(o_ref, c0, r.astype(BF16))


def _chunk_cumsum(g, c):
    pos = lax.broadcasted_iota(jnp.int32, g.shape, 0) & (c - 1)
    sh = 1
    while sh < c:
        g = g + jnp.where(pos >= sh, pltpu.roll(g, sh, 0), 0.0)
        sh *= 2
    return g


def _in_forget_body(h_ref, w_ref, lb_ref, b_ref, k_ref):
    logits = lb_ref[...]
    e = jnp.exp(logits - jnp.max(logits, axis=0, keepdims=True))
    lb_all = e[0:1, :] / jnp.sum(e, axis=0, keepdims=True)
    for c0, r in _mm_chunks(h_ref[...], w_ref, 512):
        lb = lb_all[:, c0:c0 + 512]
        f = lb + (1.0 - lb) * jax.nn.sigmoid(r)
        _heads_out(b_ref, c0, _chunk_cumsum(jnp.log(f), HGRN_CHUNK))
        _heads_out(k_ref, c0, (1.0 - f).astype(BF16))


def _in_lora_body(h_ref, w_ref, gain_ref, o_ref):
    r = jnp.dot(h_ref[...], w_ref[...], preferred_element_type=F32)
    o_ref[...] = _rms(r, gain_ref[...]).astype(BF16)


def _in_krope_body(h_ref, c_ref, a_ref, b_ref, w_ref, o_ref):
    r = jnp.dot(h_ref[...], w_ref[...], preferred_element_type=F32)
    o_ref[...] = _rope128(r, c_ref[...], a_ref[...], b_ref[...]).astype(BF16)


def _in_gate_body(h_ref, w_ref, o_ref):
    for c0, r in _mm_chunks(h_ref[...], w_ref, 512):
        o_ref[:, c0:c0 + 512] = jax.nn.sigmoid(r).astype(BF16)


def _qup_body(cq_ref, c_ref, a_ref, b_ref, w_ref, q_ref):
    scale = QK_DIM ** -0.5
    cq = cq_ref[...]
    c, a, b = c_ref[...], a_ref[...], b_ref[...]
    for c0, r in _mm_chunks(cq, w_ref, 2 * QK_PAD):
        for hh in range(2):
            head = c0 // QK_PAD + hh
            nope = r[:, hh * QK_PAD:hh * QK_PAD + HEAD_DIM]
            rope = _rope128(r[:, hh * QK_PAD + HEAD_DIM:(hh + 1) * QK_PAD], c, a, b)
            q_ref[head, :, 0:HEAD_DIM] = (nope * scale).astype(BF16)
            q_ref[head, :, HEAD_DIM:QK_PAD] = (rope * scale).astype(BF16)


def _kvup_body(ckv_ref, kr_ref, w_ref, k_ref, v_ref):
    ckv = ckv_ref[...]
    kr = kr_ref[...]
    for c0, r in _mm_chunks(ckv, w_ref, 2 * QK_PAD):
        for hh in range(2):
            head = c0 // QK_PAD + hh
            k_ref[head, :, 0:HEAD_DIM] = r[:, hh * QK_PAD:hh * QK_PAD + HEAD_DIM].astype(BF16)
            k_ref[head, :, HEAD_DIM:QK_PAD] = kr
            v_ref[head, :, 0:HEAD_DIM] = r[:, hh * QK_PAD + HEAD_DIM:(hh + 1) * QK_PAD].astype(BF16)
            v_ref[head, :, HEAD_DIM:QK_PAD] = jnp.ones((r.shape[0], HEAD_DIM), BF16)


def _attn_body(q_ref, k_ref, v_ref, o_ref, m_ref, acc_ref, p_ref, alpha_ref, *, tq, tk):
    qi = pl.program_id(1)
    nh = q_ref.shape[0]
    m_ref[...] = jnp.full(m_ref.shape, NEG, F32)
    acc_ref[...] = jnp.zeros(acc_ref.shape, F32)
    kv_rows = lambda j: pl.ds(pl.multiple_of(j * tk, tk), tk)

    def probs(j, slot, masked):
        for hh in range(nh):
            s = lax.dot_general(q_ref[hh], k_ref[hh, kv_rows(j), :], (((1,), (1,)), ((), ())),
                                preferred_element_type=F32)
            if masked:
                row = lax.broadcasted_iota(jnp.int32, s.shape, 0)
                col = lax.broadcasted_iota(jnp.int32, s.shape, 1)
                s = jnp.where(col - row <= qi * tq - j * tk, s, NEG)
            m_old = m_ref[hh]
            m_new = jnp.maximum(m_old, jnp.max(s, axis=-1, keepdims=True))
            m_ref[hh] = m_new
            alpha_ref[slot, hh] = jnp.exp(m_old - m_new)
            p_ref[slot, hh] = jnp.exp(s - m_new).astype(BF16)

    def accumulate(j, slot):
        for hh in range(nh):
            acc_ref[hh] = (alpha_ref[slot, hh] * acc_ref[hh]
                           + jnp.dot(p_ref[slot, hh], v_ref[hh, kv_rows(j), :], preferred_element_type=F32))

    def step(j, slot, masked):
        probs(j + 1, 1 - slot, masked)
        accumulate(j, slot)

    n_last = (qi * tq) // tk
    n_plain = jnp.maximum(n_last - 1, 0)
    probs(0, 0, True)

    def pair(i, carry):
        step(2 * i, 0, False)
        step(2 * i + 1, 1, False)
        return carry

    lax.fori_loop(0, n_plain // 2, pair, 0)

    @pl.when(n_plain % 2 == 1)
    def _():
        step(n_plain - 1, 0, False)

    for parity in (0, 1):
        @pl.when((n_last >= 1) & ((n_last - 1) % 2 == parity))
        def _():
            step(n_last - 1, parity, True)

    for parity in (0, 1):
        @pl.when(n_last % 2 == parity)
        def _():
            accumulate(n_last, parity)
    for hh in range(nh):
        acc = acc_ref[hh]
        o_ref[hh] = (acc[:, :HEAD_DIM] / acc[:, HEAD_DIM:]).astype(BF16)


def _attention(q, k, v):
    h, s, _ = q.shape
    tq = min(ATTN_TILE, s)
    tk = min(ATTN_KV_TILE, s)
    nh = ATTN_HEADS
    assert tk % tq == 0
    kv_bytes = nh * 2 * _nbytes((s, QK_PAD), BF16)
    scratch = [pltpu.VMEM((nh, tq, 1), F32), pltpu.VMEM((nh, tq, QK_PAD), F32),
               pltpu.VMEM((2, nh, tq, tk), BF16), pltpu.VMEM((2, nh, tq, 1), F32)]
    scratch_bytes = nh * (3 * _nbytes((tq, V7X_LANES), F32) + _nbytes((tq, QK_PAD), F32)
                          + 2 * _nbytes((tq, tk), BF16))
    return pl.pallas_call(
        functools.partial(_attn_body, tq=tq, tk=tk),
        grid=(h // nh, s // tq),
        in_specs=[pl.BlockSpec((nh, tq, QK_PAD), lambda hh, i: (hh, i, 0)),
                  pl.BlockSpec((nh, s, QK_PAD), lambda hh, i: (hh, 0, 0)),
                  pl.BlockSpec((nh, s, QK_PAD), lambda hh, i: (hh, 0, 0))],
        out_specs=pl.BlockSpec((nh, tq, HEAD_DIM), lambda hh, i: (hh, i, 0)),
        out_shape=_sds((h, s, HEAD_DIM), BF16),
        scratch_shapes=scratch,
        compiler_params=pltpu.CompilerParams(
            dimension_semantics=("arbitrary", "arbitrary"),
            vmem_limit_bytes=_vmem_limit(2 * kv_bytes, 4 * nh * _nbytes((tq, QK_PAD), BF16), scratch_bytes,
                                         4 * nh * _nbytes((tq, tk), F32))),
        name="attn",
    )(q, k, v)


def _hgrn_chunk(qf, kf, vf, b, st, ones, c):
    nt = (((1,), (1,)), ((), ()))
    vb = vf.astype(BF16)

    o = lax.dot_general((qf * jnp.exp(b)).astype(BF16), st.astype(BF16), nt, preferred_element_type=F32)

    row = lax.broadcasted_iota(jnp.int32, (c, c), 0)
    col = lax.broadcasted_iota(jnp.int32, (c, c), 1)
    scores = None
    m = c // 2
    while m >= V7X_SUBLANES:
        n = c // (2 * m)
        b3 = b.reshape(n, 2 * m, HEAD_DIM)
        mid = b3[:, m - 1:m, :]
        second = lax.broadcasted_iota(jnp.int32, b3.shape, 1) >= m
        qt = jnp.exp(jnp.where(second, b3 - mid, NEG)) * qf.reshape(b3.shape)
        kt = jnp.exp(jnp.where(second, NEG, mid - b3)) * kf.reshape(b3.shape)
        s_l = lax.dot_general(qt.reshape(c, HEAD_DIM).astype(BF16), kt.reshape(c, HEAD_DIM).astype(BF16),
                              nt, preferred_element_type=F32)
        if scores is None:
            scores = s_l
        else:
            shift = (2 * m).bit_length() - 1
            scores = jnp.where((row >> shift) == (col >> shift), s_l, scores)
        m //= 2
    o = o + jnp.dot(scores.astype(BF16), vb, preferred_element_type=F32)

    n8 = c // V7X_SUBLANES
    shp = (n8, V7X_SUBLANES, HEAD_DIM)
    b8, q8, k8, v8 = b.reshape(shp), qf.reshape(shp), kf.reshape(shp), vf.reshape(shp)
    pos = lax.broadcasted_iota(jnp.int32, shp, 1)
    for s in range(V7X_SUBLANES):
        arg = b8 - b8[:, s:s + 1, :]
        if s:
            arg = jnp.where(pos >= s, arg, NEG)
        a = jnp.exp(arg) * q8 * k8[:, s:s + 1, :]
        rs = jnp.dot(a.reshape(c, HEAD_DIM).astype(BF16), ones, preferred_element_type=F32)
        o = o + rs * jnp.broadcast_to(v8[:, s:s + 1, :], shp).reshape(c, HEAD_DIM)

    b_last = b[c - 1:c, :]
    kd = (kf * jnp.exp(b_last - b)).astype(BF16)
    st_new = st * jnp.exp(b_last) + lax.dot_general(vb, kd, (((0,), (0,)), ((), ())),
                                                    preferred_element_type=F32)
    return o, st_new


def _hgrn_body(q_ref, b_ref, k_ref, v_ref, og_ref, gain_ref, o_ref, st_ref, *, c, rows):
    @pl.when(pl.program_id(1) == 0)
    def _():
        st_ref[...] = jnp.zeros_like(st_ref)

    ones = jnp.ones((HEAD_DIM, HEAD_DIM), BF16)
    gain = gain_ref[...]

    def step(i, carry):
        sl = pl.ds(pl.multiple_of(i * c, c), c)
        for hh in range(q_ref.shape[0]):
            o, st_new = _hgrn_chunk(q_ref[hh, sl, :].astype(F32), k_ref[hh, sl, :].astype(F32),
                                    v_ref[hh, sl, :].astype(F32), b_ref[hh, sl, :], st_ref[hh], ones, c)
            st_ref[hh] = st_new
            o_ref[hh, sl, :] = (_rms(o, gain) * og_ref[hh, sl, :].astype(F32)).astype(BF16)
        return carry

    lax.fori_loop(0, rows // c, step, 0)


def _hgrn(q, b, k, v, og, gain):
    h, s, d = q.shape
    rows = min(HGRN_ROWS, s)
    c = HGRN_CHUNK
    hb = HGRN_HEADS
    spec = pl.BlockSpec((hb, rows, d), lambda hh, i: (hh, i, 0))
    return pl.pallas_call(
        functools.partial(_hgrn_body, c=c, rows=rows),
        grid=(h // hb, s // rows),
        in_specs=[spec, spec, spec, spec, spec, pl.BlockSpec((1, d), lambda hh, i: (0, 0))],
        out_specs=spec,
        out_shape=_sds((h, s, d), BF16),
        scratch_shapes=[pltpu.VMEM((hb, d, d), F32)],
        compiler_params=pltpu.CompilerParams(
            dimension_semantics=("arbitrary", "arbitrary"),
            vmem_limit_bytes=_vmem_limit(2 * 5 * hb * _nbytes((rows, d), F32), 64 * hb * _nbytes((c, d), F32))),
        name="hgrn",
    )(q, b, k, v, og, gain)


def _cat_heads(ref):
    return jnp.concatenate([ref[hh] for hh in range(ref.shape[0])], axis=-1)


def _mix_body(oa_ref, ob_ref, sa_ref, sb_ref, x_ref,
              woa_ref, wob_ref, wout_ref, gpost_ref, gate_ref, gpre_ref, sc_ref, sh_ref,
              x1_ref, h2_ref):
    ya = jnp.dot(_cat_heads(oa_ref), woa_ref[...], preferred_element_type=F32)
    yb = jnp.dot(_cat_heads(ob_ref), wob_ref[...], preferred_element_type=F32)
    merged = sa_ref[...].astype(F32) * ya + sb_ref[...].astype(F32) * yb
    y = jnp.dot(merged.astype(BF16), wout_ref[...], preferred_element_type=F32)
    x1 = x_ref[...] + gate_ref[...] * _rms(y, gpost_ref[...])
    x1_ref[...] = x1
    h2_ref[...] = (_rms(x1, gpre_ref[...]) * (1.0 + sc_ref[...]) + sh_ref[...]).astype(BF16)


def _gateup_body(h_ref, wg_ref, wu_ref, o_ref):
    h = h_ref[...]
    gte = jnp.dot(h, wg_ref[...], preferred_element_type=F32)
    up = jnp.dot(h, wu_ref[...], preferred_element_type=F32)
    o_ref[...] = (gte * jax.nn.sigmoid(gte) * up).astype(BF16)


def _gateup(h2, w_gu, d_ff):
    s, d = h2.shape
    tm = min(1024, s)
    tn = 512
    nj = d_ff // tn
    return pl.pallas_call(
        _gateup_body,
        grid=(s // tm, nj),
        in_specs=[pl.BlockSpec((tm, d), lambda i, j: (i, 0)),
                  pl.BlockSpec((d, tn), lambda i, j: (0, j)),
                  pl.BlockSpec((d, tn), lambda i, j: (0, j + nj))],
        out_specs=pl.BlockSpec((tm, tn), lambda i, j: (i, j)),
        out_shape=_sds((s, d_ff), BF16),
        compiler_params=pltpu.CompilerParams(
            dimension_semantics=("arbitrary", "arbitrary"),
            vmem_limit_bytes=_vmem_limit(2 * _nbytes((tm, d), BF16), 4 * _nbytes((d, tn), BF16),
                                         2 * _nbytes((tm, tn), BF16), 4 * _nbytes((tm, tn), F32))),
        name="gateup",
    )(h2, w_gu, w_gu)


def _down_body(a_ref, x_ref, w_ref, gpost_ref, gate_ref, o_ref):
    y = jnp.dot(a_ref[...], w_ref[...], preferred_element_type=F32)
    o_ref[...] = x_ref[...] + gate_ref[...] * _rms(y, gpost_ref[...])


def kernel(x, c, positions, w_ada, b_ada, g_pre_mix, w_in, lb_logits, g_hg_out, w_o_hg, g_q_lora, w_uq,
           g_kv_lora, w_ukv, w_o_mla, w_out, g_post_mix, g_pre_ffn, w_gate_up, w_down, g_post_ffn):
    bsz, s, d = x.shape
    assert bsz == 1 and d == N_HEADS * HEAD_DIM
    d_ff = w_down.shape[1]
    x2 = x.reshape(s, d)
    row = lambda v: v.reshape(1, -1)
    hm = lambda dt: _sds((N_HEADS, s, HEAD_DIM), dt)

    wi = w_in[0]
    o_q, o_f, o_i, o_g, o_cq, o_ckv, o_kr, o_ga, o_gb = (
        0, d, 2 * d, 3 * d, 4 * d, 4 * d + LORA, 4 * d + 2 * LORA, 4 * d + 2 * LORA + ROPE_DIM,
        5 * d + 2 * LORA + ROPE_DIM)
    w_hq = wi[:, o_q:o_f].astype(BF16)
    w_hf = wi[:, o_f:o_i].astype(BF16)
    w_hi = wi[:, o_i:o_g].astype(BF16)
    w_hg = wi[:, o_g:o_cq].astype(BF16)
    w_cq = wi[:, o_cq:o_ckv].astype(BF16)
    w_ckv = wi[:, o_ckv:o_kr].astype(BF16)
    w_kr = jnp.pad(wi[:, o_kr:o_ga], ((0, 0), (0, HEAD_DIM - ROPE_DIM))).astype(BF16)
    w_ga = wi[:, o_ga:o_gb].astype(BF16)
    w_gb = wi[:, o_gb:].astype(BF16)
    w_uq_p = jnp.pad(w_uq[0].reshape(LORA, N_HEADS, QK_DIM),
                     ((0, 0), (0, 0), (0, QK_PAD - QK_DIM))).reshape(LORA, N_HEADS * QK_PAD).astype(BF16)
    w_ukv_b = w_ukv[0].astype(BF16)
    inv_freq = ROPE_THETA ** (-jnp.arange(0, ROPE_DIM, 2, dtype=F32) / ROPE_DIM)
    inv_row = jnp.tile(inv_freq, V7X_LANES // ROPE_HALF).reshape(1, V7X_LANES)

    mod = _ada(c.reshape(d, 1), w_ada[0], row(b_ada[0]))
    sh_m, sc_m, ga_m, sh_f, sc_f, ga_f = (mod[:, i * d:(i + 1) * d] for i in range(6))

    tm = min(1024, s)
    tab = _sds((s, V7X_LANES), F32)
    rc, ra, rb = _row_call(_ropetab_body, [positions.reshape(s, 1)], [inv_row], [tab, tab, tab],
                           tm, 8 * _nbytes((tm, V7X_LANES), F32), "ropetab")

    tm = min(512, s)
    (h,) = _row_call(_prenorm_body, [x2], [row(g_pre_mix[0]), sc_m, sh_m], [_sds((s, d), BF16)],
                     tm, 3 * _nbytes((tm, d), F32), "prenorm")

    tmp = 6 * _nbytes((tm, 512), F32)
    (q_h,) = _row_call(_in_silu_body, [h], [w_hq], [hm(BF16)], tm, tmp, "in_q")
    b_h, k_h = _row_call(_in_forget_body, [h], [w_hf, lb_logits], [hm(F32), hm(BF16)], tm, tmp, "in_f")
    (v_h,) = _row_call(_in_plain_body, [h], [w_hi], [hm(BF16)], tm, tmp, "in_v")
    (og_h,) = _row_call(_in_silu_body, [h], [w_hg], [hm(BF16)], tm, tmp, "in_og")
    (cq_n,) = _row_call(_in_lora_body, [h], [w_cq, row(g_q_lora[0])], [_sds((s, LORA), BF16)], tm, tmp, "in_cq")
    (ckv_n,) = _row_call(_in_lora_body, [h], [w_ckv, row(g_kv_lora[0])], [_sds((s, LORA), BF16)], tm, tmp,
                         "in_ckv")
    (k_rope,) = _row_call(_in_krope_body, [h, rc, ra, rb], [w_kr], [_sds((s, V7X_LANES), BF16)], tm, tmp,
                          "in_kr")
    (sig_a,) = _row_call(_in_gate_body, [h], [w_ga], [_sds((s, d), BF16)], tm, tmp, "in_ga")
    (sig_b,) = _row_call(_in_gate_body, [h], [w_gb], [_sds((s, d), BF16)], tm, tmp, "in_gb")

    o_a = _hgrn(q_h, b_h, k_h, v_h, og_h, row(g_hg_out[0]))

    (q_m,) = _row_call(_qup_body, [cq_n, rc, ra, rb], [w_uq_p], [_sds((N_HEADS, s, QK_PAD), BF16)],
                       tm, tmp, "qup")
    k_m, v_m = _row_call(_kvup_body, [ckv_n, k_rope], [w_ukv_b],
                         [_sds((N_HEADS, s, QK_PAD), BF16)] * 2, tm, tmp, "kvup")
    o_b = _attention(q_m, k_m, v_m)

    tm = min(256, s)
    x1, h2 = _row_call(
        _mix_body, [o_a, o_b, sig_a, sig_b, x2],
        [w_o_hg[0].astype(BF16), w_o_mla[0].astype(BF16), w_out[0].astype(BF16),
         row(g_post_mix[0]), ga_m, row(g_pre_ffn[0]), sc_f, sh_f],
        [_sds((s, d), F32), _sds((s, d), BF16)], tm, 8 * _nbytes((tm, d), F32), "mix")

    act = _gateup(h2, w_gate_up[0].astype(BF16), d_ff)
    (out,) = _row_call(_down_body, [act, x1], [w_down[0].astype(BF16), row(g_post_ffn[0]), ga_f],
                       [_sds((s, d), F32)], tm, 4 * _nbytes((tm, d), F32), "down")
    return out.reshape(bsz, s, d)
```

```python
import functools

import jax
import jax.numpy as jnp
from jax import lax
from jax.experimental import pallas as pl
from jax.experimental.pallas import tpu as pltpu

F32 = jnp.float32
BF16 = jnp.bfloat16

V7X_VMEM_BYTES = 64 * 1024 * 1024
V7X_LANES = 128
V7X_SUBLANES = 8

EPS = 1e-6
N_HEADS = 16
HEAD_DIM = 128
ROPE_DIM = 64
ROPE_HALF = ROPE_DIM // 2
QK_DIM = HEAD_DIM + ROPE_DIM
QK_PAD = 2 * HEAD_DIM
LORA = 512
ROPE_THETA = 10000.0
NEG = -1e30
LOG2E = 1.4426950408889634

HGRN_CHUNK = 128
HGRN_ROWS = 1024
HGRN_HEADS = 8
ATTN_TILE = 512
ATTN_KV_TILE = 1024
ATTN_HEADS = 2


def _vmem_limit(*nbytes):
    need = int(sum(nbytes))
    return int(min(V7X_VMEM_BYTES - (4 << 20), need + (8 << 20)))


def _nbytes(shape, dtype):
    n = 1
    for s in shape:
        n *= s
    return n * jnp.dtype(dtype).itemsize


def _row_spec(arr_shape, tm):
    if len(arr_shape) == 2:
        return pl.BlockSpec((tm, arr_shape[1]), lambda i: (i, 0))
    return pl.BlockSpec((arr_shape[0], tm, arr_shape[2]), lambda i: (0, i, 0))


def _const_spec(arr_shape):
    nd = len(arr_shape)
    return pl.BlockSpec(tuple(arr_shape), lambda i: (0,) * nd)


def _row_call(body, row_ins, const_ins, outs, tm, temp_bytes, name, scratch=()):
    m = row_ins[0].shape[-2]
    assert m % tm == 0, (m, tm)
    in_specs = [_row_spec(a.shape, tm) for a in row_ins]
    const_bytes = 0
    const_args = []
    for cin in const_ins:
        if isinstance(cin, tuple):
            arr, width, k = cin
            in_specs.append(pl.BlockSpec((arr.shape[0], width), lambda i, k=k: (0, k),
                                         pipeline_mode=pl.Buffered(1)))
            const_bytes += _nbytes((arr.shape[0], width), arr.dtype)
        else:
            arr = cin
            in_specs.append(_const_spec(arr.shape))
            const_bytes += _nbytes(arr.shape, arr.dtype)
        const_args.append(arr)
    out_specs = [_row_spec(o.shape, tm) for o in outs]
    tile = lambda s: tuple(s[:-2]) + (tm, s[-1])
    need = (2 * sum(_nbytes(tile(a.shape), a.dtype) for a in row_ins)
            + 2 * sum(_nbytes(tile(o.shape), o.dtype) for o in outs)
            + const_bytes + sum(_nbytes(shp, dt) for shp, dt in scratch))
    res = pl.pallas_call(
        body,
        grid=(m // tm,),
        in_specs=in_specs,
        out_specs=out_specs,
        out_shape=outs,
        scratch_shapes=[pltpu.VMEM(shp, dt) for shp, dt in scratch],
        compiler_params=pltpu.CompilerParams(
            dimension_semantics=("arbitrary",),
            vmem_limit_bytes=_vmem_limit(need, temp_bytes)),
        name=name,
    )(*row_ins, *const_args)
    return res


def _weights_bf16(w_ref, wb_ref):
    @pl.when(pl.program_id(0) == 0)
    def _():
        step = 256
        for r0 in range(0, w_ref.shape[0], step):
            wb_ref[r0:r0 + step, :] = w_ref[r0:r0 + step, :].astype(BF16)


def _sds(shape, dtype):
    return jax.ShapeDtypeStruct(tuple(shape), dtype)


def _rms(y, gain):
    return y * lax.rsqrt(jnp.mean(y * y, axis=-1, keepdims=True) + EPS) * gain


def _rope128(x, c, a, b):
    return x * c + pltpu.roll(x, 96, 1) * a + pltpu.roll(x, 32, 1) * b


def _ada_body(c_ref, w_ref, b_ref, o_ref):
    cc = c_ref[...]
    sc = cc * jax.nn.sigmoid(cc)
    o_ref[...] = jnp.sum(sc * w_ref[...], axis=0, keepdims=True) + b_ref[...]


def _ada(c_col, w, b_row):
    d, n = w.shape
    tn = 1024
    return pl.pallas_call(
        _ada_body,
        grid=(n // tn,),
        in_specs=[pl.BlockSpec((d, 1), lambda j: (0, 0)),
                  pl.BlockSpec((d, tn), lambda j: (0, j)),
                  pl.BlockSpec((1, tn), lambda j: (0, j))],
        out_specs=pl.BlockSpec((1, tn), lambda j: (0, j)),
        out_shape=_sds((1, n), F32),
        compiler_params=pltpu.CompilerParams(
            dimension_semantics=("arbitrary",),
            vmem_limit_bytes=_vmem_limit(3 * _nbytes((d, tn), F32), _nbytes((d, V7X_LANES), F32))),
        name="ada",
    )(c_col, w, b_row)


def _ropetab_body(pos_ref, inv_ref, c_ref, a_ref, b_ref):
    ang = pos_ref[...].astype(F32) * inv_ref[...]
    cos = jnp.cos(ang)
    sin = jnp.sin(ang)
    lane = lax.broadcasted_iota(jnp.int32, ang.shape, 1)
    c_ref[...] = jnp.where(lane < ROPE_DIM, cos, 0.0)
    a_ref[...] = jnp.where(lane < ROPE_HALF, -sin, 0.0)
    b_ref[...] = jnp.where((lane >= ROPE_HALF) & (lane < ROPE_DIM), sin, 0.0)


def _prenorm_body(x_ref, g_ref, sc_ref, sh_ref, h_ref):
    x = x_ref[...]
    h_ref[...] = (_rms(x, g_ref[...]) * (1.0 + sc_ref[...]) + sh_ref[...]).astype(BF16)


def _heads_out(o_ref, c0, val):
    for hh in range(val.shape[1] // HEAD_DIM):
        o_ref[c0 // HEAD_DIM + hh] = val[:, hh * HEAD_DIM:(hh + 1) * HEAD_DIM]


def _mm_chunks(a, w_ref, nc):
    n = w_ref.shape[1]
    for c0 in range(0, n, nc):
        yield c0, jnp.dot(a, w_ref[:, c0:c0 + nc], preferred_element_type=F32)


def _in_silu_body(h_ref, w_ref, o_ref, wb_ref):
    _weights_bf16(w_ref, wb_ref)
    for c0, r in _mm_chunks(h_ref[...], wb_ref, 512):
        _heads_out(o_ref, c0, (r * jax.nn.sigmoid(r)).astype(BF16))


def _in_plain_body(h_ref, w_ref, o_ref, wb_ref):
    _weights_bf16(w_ref, wb_ref)
    for c0, r in _mm_chunks(h_ref[...], wb_ref, 512):
        _heads_out(o_ref, c0, r.astype(BF16))


def _chunk_cumsum(g, c):
    pos = lax.broadcasted_iota(jnp.int32, g.shape, 0) & (c - 1)
    sh = 1
    while sh < c:
        g = g + jnp.where(pos >= sh, pltpu.roll(g, sh, 0), 0.0)
        sh *= 2
    return g


def _in_forget_body(h_ref, w_ref, lb_ref, b_ref, k_ref, wb_ref):
    _weights_bf16(w_ref, wb_ref)
    logits = lb_ref[...]
    e = jnp.exp(logits - jnp.max(logits, axis=0, keepdims=True))
    lb_all = e[0:1, :] / jnp.sum(e, axis=0, keepdims=True)
    for c0, r in _mm_chunks(h_ref[...], wb_ref, 512):
        lb = lb_all[:, c0:c0 + 512]
        f = lb + (1.0 - lb) * jax.nn.sigmoid(r)
        _heads_out(b_ref, c0, _chunk_cumsum(jnp.log(f) * LOG2E, HGRN_CHUNK))
        _heads_out(k_ref, c0, (1.0 - f).astype(BF16))


def _in_lora_body(h_ref, w_ref, gain_ref, o_ref, wb_ref):
    _weights_bf16(w_ref, wb_ref)
    r = jnp.dot(h_ref[...], wb_ref[...], preferred_element_type=F32)
    o_ref[...] = _rms(r, gain_ref[...]).astype(BF16)


def _in_krope_body(h_ref, c_ref, a_ref, b_ref, w_ref, o_ref, wb_ref):
    _weights_bf16(w_ref, wb_ref)
    r = jnp.dot(h_ref[...], wb_ref[...], preferred_element_type=F32)
    o_ref[...] = _rope128(r, c_ref[...], a_ref[...], b_ref[...]).astype(BF16)


def _in_gate_body(h_ref, w_ref, o_ref):
    for c0, r in _mm_chunks(h_ref[...], w_ref, 512):
        o_ref[:, c0:c0 + 512] = jax.nn.sigmoid(r).astype(BF16)


def _qup_body(cq_ref, c_ref, a_ref, b_ref, w_ref, q_ref):
    scale = QK_DIM ** -0.5 * LOG2E
    cq = cq_ref[...]
    c, a, b = c_ref[...], a_ref[...], b_ref[...]
    for c0, r in _mm_chunks(cq, w_ref, 2 * QK_PAD):
        for hh in range(2):
            head = c0 // QK_PAD + hh
            nope = r[:, hh * QK_PAD:hh * QK_PAD + HEAD_DIM]
            rope = _rope128(r[:, hh * QK_PAD + HEAD_DIM:(hh + 1) * QK_PAD], c, a, b)
            q_ref[head, :, 0:HEAD_DIM] = (nope * scale).astype(BF16)
            q_ref[head, :, HEAD_DIM:QK_PAD] = (rope * scale).astype(BF16)


def _kvup_body(ckv_ref, kr_ref, w_ref, k_ref, v_ref):
    ckv = ckv_ref[...]
    kr = kr_ref[...]
    for c0, r in _mm_chunks(ckv, w_ref, 2 * QK_PAD):
        for hh in range(2):
            head = c0 // QK_PAD + hh
            k_ref[head, :, 0:HEAD_DIM] = r[:, hh * QK_PAD:hh * QK_PAD + HEAD_DIM].astype(BF16)
            k_ref[head, :, HEAD_DIM:QK_PAD] = kr
            v_ref[head, :, 0:HEAD_DIM] = r[:, hh * QK_PAD + HEAD_DIM:(hh + 1) * QK_PAD].astype(BF16)
            v_ref[head, :, HEAD_DIM:QK_PAD] = jnp.ones((r.shape[0], HEAD_DIM), BF16)


def _attn_body(q_ref, k_ref, v_ref, o_ref, m_ref, acc_ref, p_ref, alpha_ref, *, tq, tk):
    qi = pl.program_id(1)
    nh = q_ref.shape[0]
    m_ref[...] = jnp.full(m_ref.shape, NEG, F32)
    acc_ref[...] = jnp.zeros(acc_ref.shape, F32)
    kv_rows = lambda j: pl.ds(pl.multiple_of(j * tk, tk), tk)

    def probs(j, slot, masked):
        for hh in range(nh):
            s = lax.dot_general(q_ref[hh], k_ref[hh, kv_rows(j), :], (((1,), (1,)), ((), ())),
                                preferred_element_type=F32)
            if masked:
                row = lax.broadcasted_iota(jnp.int32, s.shape, 0)
                col = lax.broadcasted_iota(jnp.int32, s.shape, 1)
                s = jnp.where(col - row <= qi * tq - j * tk, s, NEG)
            m_old = m_ref[hh]
            m_new = jnp.maximum(m_old, jnp.max(s, axis=-1, keepdims=True))
            m_ref[hh] = m_new
            alpha_ref[slot, hh] = jnp.exp2(m_old - m_new)
            p_ref[slot, hh] = jnp.exp2(s - m_new).astype(BF16)

    def accumulate(j, slot):
        for hh in range(nh):
            acc_ref[hh] = (alpha_ref[slot, hh] * acc_ref[hh]
                           + jnp.dot(p_ref[slot, hh], v_ref[hh, kv_rows(j), :], preferred_element_type=F32))

    def step(j, slot, masked):
        probs(j + 1, 1 - slot, masked)
        accumulate(j, slot)

    n_last = (qi * tq) // tk
    n_plain = jnp.maximum(n_last - 1, 0)
    probs(0, 0, True)

    def pair(i, carry):
        step(2 * i, 0, False)
        step(2 * i + 1, 1, False)
        return carry

    lax.fori_loop(0, n_plain // 2, pair, 0)

    @pl.when(n_plain % 2 == 1)
    def _():
        step(n_plain - 1, 0, False)

    for parity in (0, 1):
        @pl.when((n_last >= 1) & ((n_last - 1) % 2 == parity))
        def _():
            step(n_last - 1, parity, True)

    for parity in (0, 1):
        @pl.when(n_last % 2 == parity)
        def _():
            accumulate(n_last, parity)
    for hh in range(nh):
        acc = acc_ref[hh]
        o_ref[hh] = (acc[:, :HEAD_DIM] / acc[:, HEAD_DIM:]).astype(BF16)


def _attention(q, k, v):
    h, s, _ = q.shape
    tq = min(ATTN_TILE, s)
    tk = min(ATTN_KV_TILE, s)
    nh = ATTN_HEADS
    assert tk % tq == 0
    kv_bytes = nh * 2 * _nbytes((s, QK_PAD), BF16)
    scratch = [pltpu.VMEM((nh, tq, 1), F32), pltpu.VMEM((nh, tq, QK_PAD), F32),
               pltpu.VMEM((2, nh, tq, tk), BF16), pltpu.VMEM((2, nh, tq, 1), F32)]
    scratch_bytes = nh * (3 * _nbytes((tq, V7X_LANES), F32) + _nbytes((tq, QK_PAD), F32)
                          + 2 * _nbytes((tq, tk), BF16))
    return pl.pallas_call(
        functools.partial(_attn_body, tq=tq, tk=tk),
        grid=(h // nh, s // tq),
        in_specs=[pl.BlockSpec((nh, tq, QK_PAD), lambda hh, i: (hh, i, 0)),
                  pl.BlockSpec((nh, s, QK_PAD), lambda hh, i: (hh, 0, 0)),
                  pl.BlockSpec((nh, s, QK_PAD), lambda hh, i: (hh, 0, 0))],
        out_specs=pl.BlockSpec((nh, tq, HEAD_DIM), lambda hh, i: (hh, i, 0)),
        out_shape=_sds((h, s, HEAD_DIM), BF16),
        scratch_shapes=scratch,
        compiler_params=pltpu.CompilerParams(
            dimension_semantics=("arbitrary", "arbitrary"),
            vmem_limit_bytes=_vmem_limit(2 * kv_bytes, 4 * nh * _nbytes((tq, QK_PAD), BF16), scratch_bytes,
                                         4 * nh * _nbytes((tq, tk), F32))),
        name="attn",
    )(q, k, v)


def _hgrn_chunk(qf, kf, vb, b, st, c):
    nt = (((1,), (1,)), ((), ()))

    o = lax.dot_general((qf * jnp.exp2(b)).astype(BF16), st.astype(BF16), nt, preferred_element_type=F32)

    row = lax.broadcasted_iota(jnp.int32, (c, c), 0)
    col = lax.broadcasted_iota(jnp.int32, (c, c), 1)
    shp8 = (c // V7X_SUBLANES, V7X_SUBLANES, HEAD_DIM)
    pos8 = lax.broadcasted_iota(jnp.int32, shp8, 1)
    scores = None
    m = c // 2
    while m >= 1:
        if m >= 2:
            shp = (c // (2 * m), 2 * m, HEAD_DIM) if m >= V7X_SUBLANES else shp8
            b3 = b.reshape(shp)
            if m >= V7X_SUBLANES:
                mid = b3[:, m - 1:m, :]
                second = lax.broadcasted_iota(jnp.int32, shp, 1) >= m
            elif m == 4:
                mid = b3[:, 3:4, :]
                second = pos8 >= 4
            else:
                mid = jnp.where(pos8 < 4, b3[:, 1:2, :], b3[:, 5:6, :])
                second = (pos8 & 3) >= 2
            d = b3 - mid
            qt = (jnp.exp2(jnp.where(second, d, NEG)) * qf.reshape(shp)).reshape(c, HEAD_DIM)
            kt = (jnp.exp2(jnp.where(second, NEG, -d)) * kf.reshape(shp)).reshape(c, HEAD_DIM)
        else:
            odd = (lax.broadcasted_iota(jnp.int32, b.shape, 0) & 1) == 1
            qt = jnp.exp2(jnp.where(odd, b - pltpu.roll(b, 1, 0), NEG)) * qf
            kt = jnp.where(odd, 0.0, kf)
        s_l = lax.dot_general(qt.astype(BF16), kt.astype(BF16), nt, preferred_element_type=F32)
        if scores is None:
            scores = s_l
        else:
            shift = (2 * m).bit_length() - 1
            scores = jnp.where((row >> shift) == (col >> shift), s_l, scores)
        m //= 2
    scores = jnp.where(row == col, jnp.sum(qf * kf, axis=-1, keepdims=True), scores)
    o = o + jnp.dot(scores.astype(BF16), vb, preferred_element_type=F32)

    b_last = b[c - 1:c, :]
    kd = (kf * jnp.exp2(b_last - b)).astype(BF16)
    st_new = st * jnp.exp2(b_last) + lax.dot_general(vb, kd, (((0,), (0,)), ((), ())),
                                                    preferred_element_type=F32)
    return o, st_new


def _hgrn_body(q_ref, b_ref, k_ref, v_ref, og_ref, gain_ref, o_ref, st_ref, *, c, rows):
    @pl.when(pl.program_id(1) == 0)
    def _():
        st_ref[...] = jnp.zeros_like(st_ref)

    gain = gain_ref[...]

    def step(i, carry):
        sl = pl.ds(pl.multiple_of(i * c, c), c)
        for hh in range(q_ref.shape[0]):
            o, st_new = _hgrn_chunk(q_ref[hh, sl, :].astype(F32), k_ref[hh, sl, :].astype(F32),
                                    v_ref[hh, sl, :], b_ref[hh, sl, :], st_ref[hh], c)
            st_ref[hh] = st_new
            o_ref[hh, sl, :] = (_rms(o, gain) * og_ref[hh, sl, :].astype(F32)).astype(BF16)
        return carry

    lax.fori_loop(0, rows // c, step, 0)


def _hgrn(q, b, k, v, og, gain):
    h, s, d = q.shape
    rows = min(HGRN_ROWS, s)
    c = HGRN_CHUNK
    hb = HGRN_HEADS
    spec = pl.BlockSpec((hb, rows, d), lambda hh, i: (hh, i, 0))
    return pl.pallas_call(
        functools.partial(_hgrn_body, c=c, rows=rows),
        grid=(h // hb, s // rows),
        in_specs=[spec, spec, spec, spec, spec, pl.BlockSpec((1, d), lambda hh, i: (0, 0))],
        out_specs=spec,
        out_shape=_sds((h, s, d), BF16),
        scratch_shapes=[pltpu.VMEM((hb, d, d), F32)],
        compiler_params=pltpu.CompilerParams(
            dimension_semantics=("arbitrary", "arbitrary"),
            vmem_limit_bytes=_vmem_limit(2 * 5 * hb * _nbytes((rows, d), F32), 64 * hb * _nbytes((c, d), F32))),
        name="hgrn",
    )(q, b, k, v, og, gain)


def _cat_heads(ref):
    return jnp.concatenate([ref[hh] for hh in range(ref.shape[0])], axis=-1)


def _mix_body(oa_ref, ob_ref, sa_ref, sb_ref, x_ref,
              woa_ref, wob_ref, wout_ref, gpost_ref, gate_ref, gpre_ref, sc_ref, sh_ref,
              x1_ref, h2_ref):
    ya = jnp.dot(_cat_heads(oa_ref), woa_ref[...], preferred_element_type=F32)
    yb = jnp.dot(_cat_heads(ob_ref), wob_ref[...], preferred_element_type=F32)
    merged = sa_ref[...].astype(F32) * ya + sb_ref[...].astype(F32) * yb
    y = jnp.dot(merged.astype(BF16), wout_ref[...], preferred_element_type=F32)
    x1 = x_ref[...] + gate_ref[...] * _rms(y, gpost_ref[...])
    x1_ref[...] = x1
    h2_ref[...] = (_rms(x1, gpre_ref[...]) * (1.0 + sc_ref[...]) + sh_ref[...]).astype(BF16)


def _gateup_body(h_ref, wg_ref, wu_ref, o_ref):
    h = h_ref[...]
    gte = jnp.dot(h, wg_ref[...].astype(BF16), preferred_element_type=F32)
    up = jnp.dot(h, wu_ref[...].astype(BF16), preferred_element_type=F32)
    o_ref[...] = (gte * jax.nn.sigmoid(gte) * up).astype(BF16)


def _gateup(h2, w_gu, d_ff):
    s, d = h2.shape
    tm = min(1024, s)
    tn = 512
    nj = d_ff // tn
    return pl.pallas_call(
        _gateup_body,
        grid=(s // tm, nj),
        in_specs=[pl.BlockSpec((tm, d), lambda i, j: (i, 0)),
                  pl.BlockSpec((d, tn), lambda i, j: (0, j)),
                  pl.BlockSpec((d, tn), lambda i, j: (0, j + nj))],
        out_specs=pl.BlockSpec((tm, tn), lambda i, j: (i, j)),
        out_shape=_sds((s, d_ff), BF16),
        compiler_params=pltpu.CompilerParams(
            dimension_semantics=("arbitrary", "arbitrary"),
            vmem_limit_bytes=_vmem_limit(2 * _nbytes((tm, d), BF16), 4 * _nbytes((d, tn), F32),
                                         2 * _nbytes((d, tn), BF16),
                                         2 * _nbytes((tm, tn), BF16), 4 * _nbytes((tm, tn), F32))),
        name="gateup",
    )(h2, w_gu, w_gu)


def _down_body(a_ref, x_ref, w_ref, gpost_ref, gate_ref, o_ref):
    y = jnp.dot(a_ref[...], w_ref[...], preferred_element_type=F32)
    o_ref[...] = x_ref[...] + gate_ref[...] * _rms(y, gpost_ref[...])


def kernel(x, c, positions, w_ada, b_ada, g_pre_mix, w_in, lb_logits, g_hg_out, w_o_hg, g_q_lora, w_uq,
           g_kv_lora, w_ukv, w_o_mla, w_out, g_post_mix, g_pre_ffn, w_gate_up, w_down, g_post_ffn):
    bsz, s, d = x.shape
    assert bsz == 1 and d == N_HEADS * HEAD_DIM
    d_ff = w_down.shape[1]
    x2 = x.reshape(s, d)
    row = lambda v: v.reshape(1, -1)
    hm = lambda dt: _sds((N_HEADS, s, HEAD_DIM), dt)

    wi = w_in.reshape(d, -1)
    o_cq, o_kr = 4 * d, 4 * d + 2 * LORA
    o_ga = o_kr + ROPE_DIM
    o_gb = o_ga + d
    w_ga = wi[:, o_ga:o_gb].astype(BF16)
    w_gb = wi[:, o_gb:].astype(BF16)
    w_uq_p = jnp.pad(w_uq[0].reshape(LORA, N_HEADS, QK_DIM),
                     ((0, 0), (0, 0), (0, QK_PAD - QK_DIM))).reshape(LORA, N_HEADS * QK_PAD).astype(BF16)
    w_ukv_b = w_ukv[0].astype(BF16)
    inv_freq = ROPE_THETA ** (-jnp.arange(0, ROPE_DIM, 2, dtype=F32) / ROPE_DIM)
    inv_row = jnp.tile(inv_freq, V7X_LANES // ROPE_HALF).reshape(1, V7X_LANES)

    mod = _ada(c.reshape(d, 1), w_ada[0], row(b_ada[0]))
    sh_m, sc_m, ga_m, sh_f, sc_f, ga_f = (mod[:, i * d:(i + 1) * d] for i in range(6))

    tm = min(1024, s)
    tab = _sds((s, V7X_LANES), F32)
    rc, ra, rb = _row_call(_ropetab_body, [positions.reshape(s, 1)], [inv_row], [tab, tab, tab],
                           tm, 8 * _nbytes((tm, V7X_LANES), F32), "ropetab")

    tm = min(512, s)
    (h,) = _row_call(_prenorm_body, [x2], [row(g_pre_mix[0]), sc_m, sh_m], [_sds((s, d), BF16)],
                     tm, 3 * _nbytes((tm, d), F32), "prenorm")

    tmp = 6 * _nbytes((tm, 512), F32)
    wsc = lambda n: [((d, n), BF16)]
    (q_h,) = _row_call(_in_silu_body, [h], [(wi, d, 0)], [hm(BF16)], tm, tmp, "in_q", wsc(d))
    b_h, k_h = _row_call(_in_forget_body, [h], [(wi, d, 1), lb_logits], [hm(F32), hm(BF16)], tm, tmp, "in_f",
                         wsc(d))
    (v_h,) = _row_call(_in_plain_body, [h], [(wi, d, 2)], [hm(BF16)], tm, tmp, "in_v", wsc(d))
    (og_h,) = _row_call(_in_silu_body, [h], [(wi, d, 3)], [hm(BF16)], tm, tmp, "in_og", wsc(d))
    (cq_n,) = _row_call(_in_lora_body, [h], [(wi, LORA, o_cq // LORA), row(g_q_lora[0])],
                        [_sds((s, LORA), BF16)], tm, tmp, "in_cq", wsc(LORA))
    (ckv_n,) = _row_call(_in_lora_body, [h], [(wi, LORA, o_cq // LORA + 1), row(g_kv_lora[0])],
                         [_sds((s, LORA), BF16)], tm, tmp, "in_ckv", wsc(LORA))
    (k_rope,) = _row_call(_in_krope_body, [h, rc, ra, rb], [(wi, V7X_LANES, o_kr // V7X_LANES)],
                          [_sds((s, V7X_LANES), BF16)], tm, tmp, "in_kr", wsc(V7X_LANES))
    (sig_a,) = _row_call(_in_gate_body, [h], [w_ga], [_sds((s, d), BF16)], tm, tmp, "in_ga")
    (sig_b,) = _row_call(_in_gate_body, [h], [w_gb], [_sds((s, d), BF16)], tm, tmp, "in_gb")

    o_a = _hgrn(q_h, b_h, k_h, v_h, og_h, row(g_hg_out[0]))

    (q_m,) = _row_call(_qup_body, [cq_n, rc, ra, rb], [w_uq_p], [_sds((N_HEADS, s, QK_PAD), BF16)],
                       tm, tmp, "qup")
    k_m, v_m = _row_call(_kvup_body, [ckv_n, k_rope], [w_ukv_b],
                         [_sds((N_HEADS, s, QK_PAD), BF16)] * 2, tm, tmp, "kvup")
    o_b = _attention(q_m, k_m, v_m)

    tm = min(256, s)
    x1, h2 = _row_call(
        _mix_body, [o_a, o_b, sig_a, sig_b, x2],
        [w_o_hg[0].astype(BF16), w_o_mla[0].astype(BF16), w_out[0].astype(BF16),
         row(g_post_mix[0]), ga_m, row(g_pre_ffn[0]), sc_f, sh_f],
        [_sds((s, d), F32), _sds((s, d), BF16)], tm, 8 * _nbytes((tm, d), F32), "mix")

    act = _gateup(h2, w_gate_up.reshape(d, 2 * d_ff), d_ff)
    (out,) = _row_call(_down_body, [act, x1], [w_down[0].astype(BF16), row(g_post_ffn[0]), ga_f],
                       [_sds((s, d), F32)], tm, 4 * _nbytes((tm, d), F32), "down")
    return out.reshape(bsz, s, d)
```

```python
import functools

import jax
import jax.numpy as jnp
from jax import lax
from jax.experimental import pallas as pl
from jax.experimental.pallas import tpu as pltpu

F32 = jnp.float32
BF16 = jnp.bfloat16

V7X_VMEM_BYTES = 64 * 1024 * 1024
V7X_LANES = 128
V7X_SUBLANES = 8

EPS = 1e-6
N_HEADS = 16
HEAD_DIM = 128
ROPE_DIM = 64
ROPE_HALF = ROPE_DIM // 2
QK_DIM = HEAD_DIM + ROPE_DIM
QK_PAD = 2 * HEAD_DIM
LORA = 512
ROPE_THETA = 10000.0
NEG = -1e30
LOG2E = 1.4426950408889634

HGRN_CHUNK = 128
HGRN_ROWS = 1024
HGRN_HEADS = 8
ATTN_TILE = 512
ATTN_KV_TILE = 1024
ATTN_HEADS = 2


def _vmem_limit(*nbytes):
    need = int(sum(nbytes))
    return int(min(V7X_VMEM_BYTES - (4 << 20), need + (8 << 20)))


def _nbytes(shape, dtype):
    n = 1
    for s in shape:
        n *= s
    return n * jnp.dtype(dtype).itemsize


def _row_spec(arr_shape, tm):
    if len(arr_shape) == 2:
        return pl.BlockSpec((tm, arr_shape[1]), lambda i: (i, 0))
    return pl.BlockSpec((arr_shape[0], tm, arr_shape[2]), lambda i: (0, i, 0))


def _const_spec(arr_shape):
    nd = len(arr_shape)
    return pl.BlockSpec(tuple(arr_shape), lambda i: (0,) * nd)


def _row_call(body, row_ins, const_ins, outs, tm, temp_bytes, name, scratch=()):
    m = row_ins[0].shape[-2]
    assert m % tm == 0, (m, tm)
    in_specs = [_row_spec(a.shape, tm) for a in row_ins]
    const_bytes = 0
    const_args = []
    for cin in const_ins:
        if isinstance(cin, tuple):
            arr, height, r0 = cin
            in_specs.append(pl.BlockSpec((None, pl.Element(height), pl.Element(arr.shape[2])),
                                         lambda i, r0=r0: (0, r0, 0), pipeline_mode=pl.Buffered(1)))
            const_bytes += _nbytes((height, arr.shape[2]), arr.dtype)
        else:
            arr = cin
            in_specs.append(_const_spec(arr.shape))
            const_bytes += _nbytes(arr.shape, arr.dtype)
        const_args.append(arr)
    out_specs = [_row_spec(o.shape, tm) for o in outs]
    tile = lambda s: tuple(s[:-2]) + (tm, s[-1])
    need = (2 * sum(_nbytes(tile(a.shape), a.dtype) for a in row_ins)
            + 2 * sum(_nbytes(tile(o.shape), o.dtype) for o in outs)
            + const_bytes + sum(_nbytes(shp, dt) for shp, dt in scratch))
    res = pl.pallas_call(
        body,
        grid=(m // tm,),
        in_specs=in_specs,
        out_specs=out_specs,
        out_shape=outs,
        scratch_shapes=[pltpu.VMEM(shp, dt) for shp, dt in scratch],
        compiler_params=pltpu.CompilerParams(
            dimension_semantics=("arbitrary",),
            vmem_limit_bytes=_vmem_limit(need, temp_bytes)),
        name=name,
    )(*row_ins, *const_args)
    return res


def _weights_bf16(w_ref, wb_ref):
    @pl.when(pl.program_id(0) == 0)
    def _():
        step = min(256, w_ref.shape[0])
        for r0 in range(0, w_ref.shape[0], step):
            wb_ref[r0:r0 + step, :] = w_ref[r0:r0 + step, :].astype(BF16)


def _mm_chunks_t(a, wt_ref, nc):
    for c0 in range(0, wt_ref.shape[0], nc):
        yield c0, lax.dot_general(a, wt_ref[c0:c0 + nc, :], (((1,), (1,)), ((), ())),
                                  preferred_element_type=F32)


def _sds(shape, dtype):
    return jax.ShapeDtypeStruct(tuple(shape), dtype)


def _rms(y, gain):
    return y * lax.rsqrt(jnp.mean(y * y, axis=-1, keepdims=True) + EPS) * gain


def _rope128(x, c, a, b):
    return x * c + pltpu.roll(x, 96, 1) * a + pltpu.roll(x, 32, 1) * b


def _ada_body(c_ref, w_ref, b_ref, o_ref):
    cc = c_ref[...]
    sc = cc * jax.nn.sigmoid(cc)
    o_ref[...] = jnp.sum(sc * w_ref[...], axis=0, keepdims=True) + b_ref[...]


def _ada(c_col, w, b_row):
    d, n = w.shape
    tn = 1024
    return pl.pallas_call(
        _ada_body,
        grid=(n // tn,),
        in_specs=[pl.BlockSpec((d, 1), lambda j: (0, 0)),
                  pl.BlockSpec((d, tn), lambda j: (0, j)),
                  pl.BlockSpec((1, tn), lambda j: (0, j))],
        out_specs=pl.BlockSpec((1, tn), lambda j: (0, j)),
        out_shape=_sds((1, n), F32),
        compiler_params=pltpu.CompilerParams(
            dimension_semantics=("arbitrary",),
            vmem_limit_bytes=_vmem_limit(3 * _nbytes((d, tn), F32), _nbytes((d, V7X_LANES), F32))),
        name="ada",
    )(c_col, w, b_row)


def _ropetab_body(pos_ref, inv_ref, c_ref, a_ref, b_ref):
    ang = pos_ref[...].astype(F32) * inv_ref[...]
    cos = jnp.cos(ang)
    sin = jnp.sin(ang)
    lane = lax.broadcasted_iota(jnp.int32, ang.shape, 1)
    c_ref[...] = jnp.where(lane < ROPE_DIM, cos, 0.0)
    a_ref[...] = jnp.where(lane < ROPE_HALF, -sin, 0.0)
    b_ref[...] = jnp.where((lane >= ROPE_HALF) & (lane < ROPE_DIM), sin, 0.0)


def _prenorm_body(x_ref, g_ref, sc_ref, sh_ref, h_ref):
    x = x_ref[...]
    h_ref[...] = (_rms(x, g_ref[...]) * (1.0 + sc_ref[...]) + sh_ref[...]).astype(BF16)


def _heads_out(o_ref, c0, val):
    for hh in range(val.shape[1] // HEAD_DIM):
        o_ref[c0 // HEAD_DIM + hh] = val[:, hh * HEAD_DIM:(hh + 1) * HEAD_DIM]


def _mm_chunks(a, w_ref, nc):
    n = w_ref.shape[1]
    for c0 in range(0, n, nc):
        yield c0, jnp.dot(a, w_ref[:, c0:c0 + nc], preferred_element_type=F32)


def _in_silu_body(h_ref, w_ref, o_ref, wb_ref):
    _weights_bf16(w_ref, wb_ref)
    for c0, r in _mm_chunks_t(h_ref[...], wb_ref, 512):
        _heads_out(o_ref, c0, (r * jax.nn.sigmoid(r)).astype(BF16))


def _in_plain_body(h_ref, w_ref, o_ref, wb_ref):
    _weights_bf16(w_ref, wb_ref)
    for c0, r in _mm_chunks_t(h_ref[...], wb_ref, 512):
        _heads_out(o_ref, c0, r.astype(BF16))


def _chunk_cumsum(g, c):
    pos = lax.broadcasted_iota(jnp.int32, g.shape, 0) & (c - 1)
    sh = 1
    while sh < c:
        g = g + jnp.where(pos >= sh, pltpu.roll(g, sh, 0), 0.0)
        sh *= 2
    return g


def _in_forget_body(h_ref, w_ref, lb_ref, b_ref, k_ref, wb_ref):
    _weights_bf16(w_ref, wb_ref)
    logits = lb_ref[...]
    e = jnp.exp(logits - jnp.max(logits, axis=0, keepdims=True))
    lb_all = e[0:1, :] / jnp.sum(e, axis=0, keepdims=True)
    for c0, r in _mm_chunks_t(h_ref[...], wb_ref, 512):
        lb = lb_all[:, c0:c0 + 512]
        f = lb + (1.0 - lb) * jax.nn.sigmoid(r)
        _heads_out(b_ref, c0, _chunk_cumsum(jnp.log(f) * LOG2E, HGRN_CHUNK))
        _heads_out(k_ref, c0, (1.0 - f).astype(BF16))


def _in_lora_body(h_ref, w_ref, gain_ref, o_ref, wb_ref):
    _weights_bf16(w_ref, wb_ref)
    ((_, r),) = _mm_chunks_t(h_ref[...], wb_ref, wb_ref.shape[0])
    o_ref[...] = _rms(r, gain_ref[...]).astype(BF16)


def _in_krope_body(h_ref, c_ref, a_ref, b_ref, w_ref, o_ref, wb_ref):
    _weights_bf16(w_ref, wb_ref)
    ((_, r),) = _mm_chunks_t(h_ref[...], wb_ref, wb_ref.shape[0])
    o_ref[...] = _rope128(r, c_ref[...], a_ref[...], b_ref[...]).astype(BF16)


def _in_gate_body(h_ref, w_ref, o_ref, wb_ref):
    _weights_bf16(w_ref, wb_ref)
    for c0, r in _mm_chunks_t(h_ref[...], wb_ref, 512):
        o_ref[:, c0:c0 + 512] = jax.nn.sigmoid(r).astype(BF16)


def _qup_body(cq_ref, c_ref, a_ref, b_ref, w_ref, q_ref):
    scale = QK_DIM ** -0.5 * LOG2E
    cq = cq_ref[...]
    c, a, b = c_ref[...], a_ref[...], b_ref[...]
    for c0, r in _mm_chunks(cq, w_ref, 2 * QK_PAD):
        for hh in range(2):
            head = c0 // QK_PAD + hh
            nope = r[:, hh * QK_PAD:hh * QK_PAD + HEAD_DIM]
            rope = _rope128(r[:, hh * QK_PAD + HEAD_DIM:(hh + 1) * QK_PAD], c, a, b)
            q_ref[head, :, 0:HEAD_DIM] = (nope * scale).astype(BF16)
            q_ref[head, :, HEAD_DIM:QK_PAD] = (rope * scale).astype(BF16)


def _kvup_body(ckv_ref, kr_ref, w_ref, k_ref, v_ref):
    ckv = ckv_ref[...]
    kr = kr_ref[...]
    for c0, r in _mm_chunks(ckv, w_ref, 2 * QK_PAD):
        for hh in range(2):
            head = c0 // QK_PAD + hh
            k_ref[head, :, 0:HEAD_DIM] = r[:, hh * QK_PAD:hh * QK_PAD + HEAD_DIM].astype(BF16)
            k_ref[head, :, HEAD_DIM:QK_PAD] = kr
            v_ref[head, :, 0:HEAD_DIM] = r[:, hh * QK_PAD + HEAD_DIM:(hh + 1) * QK_PAD].astype(BF16)
            v_ref[head, :, HEAD_DIM:QK_PAD] = jnp.ones((r.shape[0], HEAD_DIM), BF16)


def _attn_body(q_ref, k_ref, v_ref, o_ref, m_ref, acc_ref, p_ref, alpha_ref, *, tq, tk):
    qi = pl.program_id(1)
    nh = q_ref.shape[0]
    m_ref[...] = jnp.full(m_ref.shape, NEG, F32)
    acc_ref[...] = jnp.zeros(acc_ref.shape, F32)
    kv_rows = lambda j: pl.ds(pl.multiple_of(j * tk, tk), tk)

    def probs(j, slot, masked):
        for hh in range(nh):
            s = lax.dot_general(q_ref[hh], k_ref[hh, kv_rows(j), :], (((1,), (1,)), ((), ())),
                                preferred_element_type=F32)
            if masked:
                row = lax.broadcasted_iota(jnp.int32, s.shape, 0)
                col = lax.broadcasted_iota(jnp.int32, s.shape, 1)
                s = jnp.where(col - row <= qi * tq - j * tk, s, NEG)
            m_old = m_ref[hh]
            m_new = jnp.maximum(m_old, jnp.max(s, axis=-1, keepdims=True))
            m_ref[hh] = m_new
            alpha_ref[slot, hh] = jnp.exp2(m_old - m_new)
            p_ref[slot, hh] = jnp.exp2(s - m_new).astype(BF16)

    def accumulate(j, slot):
        for hh in range(nh):
            acc_ref[hh] = (alpha_ref[slot, hh] * acc_ref[hh]
                           + jnp.dot(p_ref[slot, hh], v_ref[hh, kv_rows(j), :], preferred_element_type=F32))

    def step(j, slot, masked):
        probs(j + 1, 1 - slot, masked)
        accumulate(j, slot)

    n_last = (qi * tq) // tk
    n_plain = jnp.maximum(n_last - 1, 0)
    probs(0, 0, True)

    def pair(i, carry):
        step(2 * i, 0, False)
        step(2 * i + 1, 1, False)
        return carry

    lax.fori_loop(0, n_plain // 2, pair, 0)

    @pl.when(n_plain % 2 == 1)
    def _():
        step(n_plain - 1, 0, False)

    for parity in (0, 1):
        @pl.when((n_last >= 1) & ((n_last - 1) % 2 == parity))
        def _():
            step(n_last - 1, parity, True)

    for parity in (0, 1):
        @pl.when(n_last % 2 == parity)
        def _():
            accumulate(n_last, parity)
    for hh in range(nh):
        acc = acc_ref[hh]
        o_ref[hh] = (acc[:, :HEAD_DIM] / acc[:, HEAD_DIM:]).astype(BF16)


def _attention(q, k, v):
    h, s, _ = q.shape
    tq = min(ATTN_TILE, s)
    tk = min(ATTN_KV_TILE, s)
    nh = ATTN_HEADS
    assert tk % tq == 0
    kv_bytes = nh * 2 * _nbytes((s, QK_PAD), BF16)
    scratch = [pltpu.VMEM((nh, tq, 1), F32), pltpu.VMEM((nh, tq, QK_PAD), F32),
               pltpu.VMEM((2, nh, tq, tk), BF16), pltpu.VMEM((2, nh, tq, 1), F32)]
    scratch_bytes = nh * (3 * _nbytes((tq, V7X_LANES), F32) + _nbytes((tq, QK_PAD), F32)
                          + 2 * _nbytes((tq, tk), BF16))
    return pl.pallas_call(
        functools.partial(_attn_body, tq=tq, tk=tk),
        grid=(h // nh, s // tq),
        in_specs=[pl.BlockSpec((nh, tq, QK_PAD), lambda hh, i: (hh, i, 0)),
                  pl.BlockSpec((nh, s, QK_PAD), lambda hh, i: (hh, 0, 0)),
                  pl.BlockSpec((nh, s, QK_PAD), lambda hh, i: (hh, 0, 0))],
        out_specs=pl.BlockSpec((nh, tq, HEAD_DIM), lambda hh, i: (hh, i, 0)),
        out_shape=_sds((h, s, HEAD_DIM), BF16),
        scratch_shapes=scratch,
        compiler_params=pltpu.CompilerParams(
            dimension_semantics=("arbitrary", "arbitrary"),
            vmem_limit_bytes=_vmem_limit(2 * kv_bytes, 4 * nh * _nbytes((tq, QK_PAD), BF16), scratch_bytes,
                                         4 * nh * _nbytes((tq, tk), F32))),
        name="attn",
    )(q, k, v)


def _hgrn_chunk(qf, kf, vb, b, st, c):
    nt = (((1,), (1,)), ((), ()))

    o = lax.dot_general((qf * jnp.exp2(b)).astype(BF16), st.astype(BF16), nt, preferred_element_type=F32)

    row = lax.broadcasted_iota(jnp.int32, (c, c), 0)
    col = lax.broadcasted_iota(jnp.int32, (c, c), 1)
    shp8 = (c // V7X_SUBLANES, V7X_SUBLANES, HEAD_DIM)
    pos8 = lax.broadcasted_iota(jnp.int32, shp8, 1)
    scores = None
    m = c // 2
    while m >= 1:
        if m >= 2:
            shp = (c // (2 * m), 2 * m, HEAD_DIM) if m >= V7X_SUBLANES else shp8
            b3 = b.reshape(shp)
            if m >= V7X_SUBLANES:
                mid = b3[:, m - 1:m, :]
                second = lax.broadcasted_iota(jnp.int32, shp, 1) >= m
            elif m == 4:
                mid = b3[:, 3:4, :]
                second = pos8 >= 4
            else:
                mid = jnp.where(pos8 < 4, b3[:, 1:2, :], b3[:, 5:6, :])
                second = (pos8 & 3) >= 2
            d = b3 - mid
            qt = (jnp.exp2(jnp.where(second, d, NEG)) * qf.reshape(shp)).reshape(c, HEAD_DIM)
            kt = (jnp.exp2(jnp.where(second, NEG, -d)) * kf.reshape(shp)).reshape(c, HEAD_DIM)
        else:
            odd = (lax.broadcasted_iota(jnp.int32, b.shape, 0) & 1) == 1
            qt = jnp.exp2(jnp.where(odd, b - pltpu.roll(b, 1, 0), NEG)) * qf
            kt = jnp.where(odd, 0.0, kf)
        s_l = lax.dot_general(qt.astype(BF16), kt.astype(BF16), nt, preferred_element_type=F32)
        if scores is None:
            scores = s_l
        else:
            shift = (2 * m).bit_length() - 1
            scores = jnp.where((row >> shift) == (col >> shift), s_l, scores)
        m //= 2
    scores = jnp.where(row == col, jnp.sum(qf * kf, axis=-1, keepdims=True), scores)
    o = o + jnp.dot(scores.astype(BF16), vb, preferred_element_type=F32)

    b_last = b[c - 1:c, :]
    kd = (kf * jnp.exp2(b_last - b)).astype(BF16)
    st_new = st * jnp.exp2(b_last) + lax.dot_general(vb, kd, (((0,), (0,)), ((), ())),
                                                    preferred_element_type=F32)
    return o, st_new


def _hgrn_body(q_ref, b_ref, k_ref, v_ref, og_ref, gain_ref, o_ref, st_ref, *, c, rows):
    @pl.when(pl.program_id(1) == 0)
    def _():
        st_ref[...] = jnp.zeros_like(st_ref)

    gain = gain_ref[...]

    def step(i, carry):
        sl = pl.ds(pl.multiple_of(i * c, c), c)
        for hh in range(q_ref.shape[0]):
            o, st_new = _hgrn_chunk(q_ref[hh, sl, :].astype(F32), k_ref[hh, sl, :].astype(F32),
                                    v_ref[hh, sl, :], b_ref[hh, sl, :], st_ref[hh], c)
            st_ref[hh] = st_new
            o_ref[hh, sl, :] = (_rms(o, gain) * og_ref[hh, sl, :].astype(F32)).astype(BF16)
        return carry

    lax.fori_loop(0, rows // c, step, 0)


def _hgrn(q, b, k, v, og, gain):
    h, s, d = q.shape
    rows = min(HGRN_ROWS, s)
    c = HGRN_CHUNK
    hb = HGRN_HEADS
    spec = pl.BlockSpec((hb, rows, d), lambda hh, i: (hh, i, 0))
    return pl.pallas_call(
        functools.partial(_hgrn_body, c=c, rows=rows),
        grid=(h // hb, s // rows),
        in_specs=[spec, spec, spec, spec, spec, pl.BlockSpec((1, d), lambda hh, i: (0, 0))],
        out_specs=spec,
        out_shape=_sds((h, s, d), BF16),
        scratch_shapes=[pltpu.VMEM((hb, d, d), F32)],
        compiler_params=pltpu.CompilerParams(
            dimension_semantics=("arbitrary", "arbitrary"),
            vmem_limit_bytes=_vmem_limit(2 * 5 * hb * _nbytes((rows, d), F32), 64 * hb * _nbytes((c, d), F32))),
        name="hgrn",
    )(q, b, k, v, og, gain)


def _cat_heads(ref):
    return jnp.concatenate([ref[hh] for hh in range(ref.shape[0])], axis=-1)


def _mix_body(oa_ref, ob_ref, sa_ref, sb_ref, x_ref,
              woa_ref, wob_ref, wout_ref, gpost_ref, gate_ref, gpre_ref, sc_ref, sh_ref,
              x1_ref, h2_ref):
    ya = jnp.dot(_cat_heads(oa_ref), woa_ref[...], preferred_element_type=F32)
    yb = jnp.dot(_cat_heads(ob_ref), wob_ref[...], preferred_element_type=F32)
    merged = sa_ref[...].astype(F32) * ya + sb_ref[...].astype(F32) * yb
    y = jnp.dot(merged.astype(BF16), wout_ref[...], preferred_element_type=F32)
    x1 = x_ref[...] + gate_ref[...] * _rms(y, gpost_ref[...])
    x1_ref[...] = x1
    h2_ref[...] = (_rms(x1, gpre_ref[...]) * (1.0 + sc_ref[...]) + sh_ref[...]).astype(BF16)


def _gateup_body(h_ref, wg_ref, wu_ref, o_ref):
    h = h_ref[...]
    gte = jnp.dot(h, wg_ref[...].astype(BF16), preferred_element_type=F32)
    up = jnp.dot(h, wu_ref[...].astype(BF16), preferred_element_type=F32)
    o_ref[...] = (gte * jax.nn.sigmoid(gte) * up).astype(BF16)


def _gateup(h2, w_gu, d_ff):
    s, d = h2.shape
    tm = min(1024, s)
    tn = 512
    nj = d_ff // tn
    return pl.pallas_call(
        _gateup_body,
        grid=(s // tm, nj),
        in_specs=[pl.BlockSpec((tm, d), lambda i, j: (i, 0)),
                  pl.BlockSpec((None, d, tn), lambda i, j: (0, 0, j)),
                  pl.BlockSpec((None, d, tn), lambda i, j: (0, 0, j + nj))],
        out_specs=pl.BlockSpec((tm, tn), lambda i, j: (i, j)),
        out_shape=_sds((s, d_ff), BF16),
        compiler_params=pltpu.CompilerParams(
            dimension_semantics=("arbitrary", "arbitrary"),
            vmem_limit_bytes=_vmem_limit(2 * _nbytes((tm, d), BF16), 4 * _nbytes((d, tn), F32),
                                         2 * _nbytes((d, tn), BF16),
                                         2 * _nbytes((tm, tn), BF16), 4 * _nbytes((tm, tn), F32))),
        name="gateup",
    )(h2, w_gu, w_gu)


def _down_body(a_ref, x_ref, w_ref, gpost_ref, gate_ref, o_ref):
    y = jnp.dot(a_ref[...], w_ref[...], preferred_element_type=F32)
    o_ref[...] = x_ref[...] + gate_ref[...] * _rms(y, gpost_ref[...])


def kernel(x, c, positions, w_ada, b_ada, g_pre_mix, w_in, lb_logits, g_hg_out, w_o_hg, g_q_lora, w_uq,
           g_kv_lora, w_ukv, w_o_mla, w_out, g_post_mix, g_pre_ffn, w_gate_up, w_down, g_post_ffn):
    bsz, s, d = x.shape
    assert bsz == 1 and d == N_HEADS * HEAD_DIM
    d_ff = w_down.shape[1]
    x2 = x.reshape(s, d)
    row = lambda v: v.reshape(1, -1)
    hm = lambda dt: _sds((N_HEADS, s, HEAD_DIM), dt)

    w_in_t = jnp.swapaxes(w_in, 1, 2)
    o_cq, o_kr = 4 * d, 4 * d + 2 * LORA
    o_ga = o_kr + ROPE_DIM
    o_gb = o_ga + d
    w_uq_p = jnp.pad(w_uq[0].reshape(LORA, N_HEADS, QK_DIM),
                     ((0, 0), (0, 0), (0, QK_PAD - QK_DIM))).reshape(LORA, N_HEADS * QK_PAD).astype(BF16)
    w_ukv_b = w_ukv[0].astype(BF16)
    inv_freq = ROPE_THETA ** (-jnp.arange(0, ROPE_DIM, 2, dtype=F32) / ROPE_DIM)
    inv_row = jnp.tile(inv_freq, V7X_LANES // ROPE_HALF).reshape(1, V7X_LANES)

    mod = _ada(c.reshape(d, 1), w_ada[0], row(b_ada[0]))
    sh_m, sc_m, ga_m, sh_f, sc_f, ga_f = (mod[:, i * d:(i + 1) * d] for i in range(6))

    tm = min(1024, s)
    tab = _sds((s, V7X_LANES), F32)
    rc, ra, rb = _row_call(_ropetab_body, [positions.reshape(s, 1)], [inv_row], [tab, tab, tab],
                           tm, 8 * _nbytes((tm, V7X_LANES), F32), "ropetab")

    tm = min(512, s)
    (h,) = _row_call(_prenorm_body, [x2], [row(g_pre_mix[0]), sc_m, sh_m], [_sds((s, d), BF16)],
                     tm, 3 * _nbytes((tm, d), F32), "prenorm")

    tmp = 6 * _nbytes((tm, 512), F32)
    wsc = lambda n: [((n, d), BF16)]
    wt = lambda r0, n: (w_in_t, n, r0)
    (q_h,) = _row_call(_in_silu_body, [h], [wt(0, d)], [hm(BF16)], tm, tmp, "in_q", wsc(d))
    b_h, k_h = _row_call(_in_forget_body, [h], [wt(d, d), lb_logits], [hm(F32), hm(BF16)], tm, tmp, "in_f",
                         wsc(d))
    (v_h,) = _row_call(_in_plain_body, [h], [wt(2 * d, d)], [hm(BF16)], tm, tmp, "in_v", wsc(d))
    (og_h,) = _row_call(_in_silu_body, [h], [wt(3 * d, d)], [hm(BF16)], tm, tmp, "in_og", wsc(d))
    (cq_n,) = _row_call(_in_lora_body, [h], [wt(o_cq, LORA), row(g_q_lora[0])],
                        [_sds((s, LORA), BF16)], tm, tmp, "in_cq", wsc(LORA))
    (ckv_n,) = _row_call(_in_lora_body, [h], [wt(o_cq + LORA, LORA), row(g_kv_lora[0])],
                         [_sds((s, LORA), BF16)], tm, tmp, "in_ckv", wsc(LORA))
    (k_rope,) = _row_call(_in_krope_body, [h, rc, ra, rb], [wt(o_kr, V7X_LANES)],
                          [_sds((s, V7X_LANES), BF16)], tm, tmp, "in_kr", wsc(V7X_LANES))
    (sig_a,) = _row_call(_in_gate_body, [h], [wt(o_ga, d)], [_sds((s, d), BF16)], tm, tmp, "in_ga", wsc(d))
    (sig_b,) = _row_call(_in_gate_body, [h], [wt(o_gb, d)], [_sds((s, d), BF16)], tm, tmp, "in_gb", wsc(d))

    o_a = _hgrn(q_h, b_h, k_h, v_h, og_h, row(g_hg_out[0]))

    (q_m,) = _row_call(_qup_body, [cq_n, rc, ra, rb], [w_uq_p], [_sds((N_HEADS, s, QK_PAD), BF16)],
                       tm, tmp, "qup")
    k_m, v_m = _row_call(_kvup_body, [ckv_n, k_rope], [w_ukv_b],
                         [_sds((N_HEADS, s, QK_PAD), BF16)] * 2, tm, tmp, "kvup")
    o_b = _attention(q_m, k_m, v_m)

    tm = min(256, s)
    x1, h2 = _row_call(
        _mix_body, [o_a, o_b, sig_a, sig_b, x2],
        [w_o_hg[0].astype(BF16), w_o_mla[0].astype(BF16), w_out[0].astype(BF16),
         row(g_post_mix[0]), ga_m, row(g_pre_ffn[0]), sc_f, sh_f],
        [_sds((s, d), F32), _sds((s, d), BF16)], tm, 8 * _nbytes((tm, d), F32), "mix")

    act = _gateup(h2, w_gate_up, d_ff)
    (out,) = _row_call(_down_body, [act, x1], [w_down[0].astype(BF16), row(g_post_ffn[0]), ga_f],
                       [_sds((s, d), F32)], tm, 4 * _nbytes((tm, d), F32), "down")
    return out.reshape(bsz, s, d)
```

```python
import functools

import jax
import jax.numpy as jnp
from jax import lax
from jax.experimental import pallas as pl
from jax.experimental.pallas import tpu as pltpu

F32 = jnp.float32
BF16 = jnp.bfloat16

V7X_VMEM_BYTES = 64 * 1024 * 1024
V7X_LANES = 128
V7X_SUBLANES = 8

EPS = 1e-6
N_HEADS = 16
HEAD_DIM = 128
ROPE_DIM = 64
ROPE_HALF = ROPE_DIM // 2
QK_DIM = HEAD_DIM + ROPE_DIM
QK_PAD = 2 * HEAD_DIM
LORA = 512
ROPE_THETA = 10000.0
NEG = -1e30
LOG2E = 1.4426950408889634

HGRN_CHUNK = 128
HGRN_ROWS = 1024
HGRN_HEADS = 8
ATTN_TILE = 512
ATTN_KV_TILE = 1024
ATTN_HEADS = 4


def _vmem_limit(*nbytes):
    need = int(sum(nbytes))
    return int(min(V7X_VMEM_BYTES - (4 << 20), need + (8 << 20)))


def _nbytes(shape, dtype):
    n = 1
    for s in shape:
        n *= s
    return n * jnp.dtype(dtype).itemsize


def _row_spec(arr_shape, tm):
    if len(arr_shape) == 2:
        return pl.BlockSpec((tm, arr_shape[1]), lambda i: (i, 0))
    return pl.BlockSpec((arr_shape[0], tm, arr_shape[2]), lambda i: (0, i, 0))


def _const_spec(arr_shape):
    nd = len(arr_shape)
    return pl.BlockSpec(tuple(arr_shape), lambda i: (0,) * nd)


def _row_call(body, row_ins, const_ins, outs, tm, temp_bytes, name, scratch=()):
    m = row_ins[0].shape[-2]
    assert m % tm == 0, (m, tm)
    in_specs = [_row_spec(a.shape, tm) for a in row_ins]
    const_bytes = 0
    const_args = []
    for cin in const_ins:
        if isinstance(cin, tuple):
            arr, height, r0 = cin
            in_specs.append(pl.BlockSpec((None, pl.Element(height), pl.Element(arr.shape[2])),
                                         lambda i, r0=r0: (0, r0, 0), pipeline_mode=pl.Buffered(1)))
            const_bytes += _nbytes((height, arr.shape[2]), arr.dtype)
        else:
            arr = cin
            in_specs.append(_const_spec(arr.shape))
            const_bytes += _nbytes(arr.shape, arr.dtype)
        const_args.append(arr)
    out_specs = [_row_spec(o.shape, tm) for o in outs]
    tile = lambda s: tuple(s[:-2]) + (tm, s[-1])
    need = (2 * sum(_nbytes(tile(a.shape), a.dtype) for a in row_ins)
            + 2 * sum(_nbytes(tile(o.shape), o.dtype) for o in outs)
            + const_bytes + sum(_nbytes(shp, dt) for shp, dt in scratch))
    res = pl.pallas_call(
        body,
        grid=(m // tm,),
        in_specs=in_specs,
        out_specs=out_specs,
        out_shape=outs,
        scratch_shapes=[pltpu.VMEM(shp, dt) for shp, dt in scratch],
        compiler_params=pltpu.CompilerParams(
            dimension_semantics=("arbitrary",),
            vmem_limit_bytes=_vmem_limit(need, temp_bytes)),
        name=name,
    )(*row_ins, *const_args)
    return res


def _weights_bf16(w_ref, wb_ref):
    @pl.when(pl.program_id(0) == 0)
    def _():
        step = min(256, w_ref.shape[0])
        for r0 in range(0, w_ref.shape[0], step):
            wb_ref[r0:r0 + step, :] = w_ref[r0:r0 + step, :].astype(BF16)


def _mm_chunks_t(a, wt_ref, nc):
    for c0 in range(0, wt_ref.shape[0], nc):
        yield c0, lax.dot_general(a, wt_ref[c0:c0 + nc, :], (((1,), (1,)), ((), ())),
                                  preferred_element_type=F32)


def _sds(shape, dtype):
    return jax.ShapeDtypeStruct(tuple(shape), dtype)


def _rms(y, gain):
    return y * lax.rsqrt(jnp.mean(y * y, axis=-1, keepdims=True) + EPS) * gain


def _rope128(x, c, a, b):
    return x * c + pltpu.roll(x, 96, 1) * a + pltpu.roll(x, 32, 1) * b


def _ada_body(c_ref, w_ref, b_ref, o_ref):
    cc = c_ref[...]
    sc = cc * jax.nn.sigmoid(cc)
    o_ref[...] = jnp.sum(sc * w_ref[...], axis=0, keepdims=True) + b_ref[...]


def _ada(c_col, w, b_row):
    d, n = w.shape
    tn = 1024
    return pl.pallas_call(
        _ada_body,
        grid=(n // tn,),
        in_specs=[pl.BlockSpec((d, 1), lambda j: (0, 0)),
                  pl.BlockSpec((d, tn), lambda j: (0, j)),
                  pl.BlockSpec((1, tn), lambda j: (0, j))],
        out_specs=pl.BlockSpec((1, tn), lambda j: (0, j)),
        out_shape=_sds((1, n), F32),
        compiler_params=pltpu.CompilerParams(
            dimension_semantics=("arbitrary",),
            vmem_limit_bytes=_vmem_limit(3 * _nbytes((d, tn), F32), _nbytes((d, V7X_LANES), F32))),
        name="ada",
    )(c_col, w, b_row)


def _ropetab_body(pos_ref, inv_ref, c_ref, a_ref, b_ref):
    ang = pos_ref[...].astype(F32) * inv_ref[...]
    cos = jnp.cos(ang)
    sin = jnp.sin(ang)
    lane = lax.broadcasted_iota(jnp.int32, ang.shape, 1)
    c_ref[...] = jnp.where(lane < ROPE_DIM, cos, 0.0)
    a_ref[...] = jnp.where(lane < ROPE_HALF, -sin, 0.0)
    b_ref[...] = jnp.where((lane >= ROPE_HALF) & (lane < ROPE_DIM), sin, 0.0)


def _prenorm_body(x_ref, g_ref, sc_ref, sh_ref, h_ref):
    x = x_ref[...]
    h_ref[...] = (_rms(x, g_ref[...]) * (1.0 + sc_ref[...]) + sh_ref[...]).astype(BF16)


def _heads_out(o_ref, c0, val):
    for hh in range(val.shape[1] // HEAD_DIM):
        o_ref[c0 // HEAD_DIM + hh] = val[:, hh * HEAD_DIM:(hh + 1) * HEAD_DIM]


def _mm_chunks(a, w_ref, nc):
    n = w_ref.shape[1]
    for c0 in range(0, n, nc):
        yield c0, jnp.dot(a, w_ref[:, c0:c0 + nc], preferred_element_type=F32)


def _in_silu_body(h_ref, w_ref, o_ref, wb_ref):
    _weights_bf16(w_ref, wb_ref)
    for c0, r in _mm_chunks_t(h_ref[...], wb_ref, 512):
        _heads_out(o_ref, c0, (r * jax.nn.sigmoid(r)).astype(BF16))


def _in_plain_body(h_ref, w_ref, o_ref, wb_ref):
    _weights_bf16(w_ref, wb_ref)
    for c0, r in _mm_chunks_t(h_ref[...], wb_ref, 512):
        _heads_out(o_ref, c0, r.astype(BF16))


def _chunk_cumsum(g, c):
    pos = lax.broadcasted_iota(jnp.int32, g.shape, 0) & (c - 1)
    sh = 1
    while sh < c:
        g = g + jnp.where(pos >= sh, pltpu.roll(g, sh, 0), 0.0)
        sh *= 2
    return g


def _in_forget_body(h_ref, w_ref, lb_ref, b_ref, k_ref, wb_ref):
    _weights_bf16(w_ref, wb_ref)
    logits = lb_ref[...]
    e = jnp.exp(logits - jnp.max(logits, axis=0, keepdims=True))
    lb_all = e[0:1, :] / jnp.sum(e, axis=0, keepdims=True)
    for c0, r in _mm_chunks_t(h_ref[...], wb_ref, 512):
        lb = lb_all[:, c0:c0 + 512]
        f = lb + (1.0 - lb) * jax.nn.sigmoid(r)
        _heads_out(b_ref, c0, _chunk_cumsum(jnp.log(f) * LOG2E, HGRN_CHUNK))
        _heads_out(k_ref, c0, (1.0 - f).astype(BF16))


def _in_lora_body(h_ref, w_ref, gain_ref, o_ref, wb_ref):
    _weights_bf16(w_ref, wb_ref)
    ((_, r),) = _mm_chunks_t(h_ref[...], wb_ref, wb_ref.shape[0])
    o_ref[...] = _rms(r, gain_ref[...]).astype(BF16)


def _in_krope_body(h_ref, c_ref, a_ref, b_ref, w_ref, o_ref, wb_ref):
    _weights_bf16(w_ref, wb_ref)
    ((_, r),) = _mm_chunks_t(h_ref[...], wb_ref, wb_ref.shape[0])
    o_ref[...] = _rope128(r, c_ref[...], a_ref[...], b_ref[...]).astype(BF16)


def _in_gate_body(h_ref, w_ref, o_ref, wb_ref):
    _weights_bf16(w_ref, wb_ref)
    for c0, r in _mm_chunks_t(h_ref[...], wb_ref, 512):
        o_ref[:, c0:c0 + 512] = jax.nn.sigmoid(r).astype(BF16)


def _qup_body(cq_ref, c_ref, a_ref, b_ref, w_ref, q_ref):
    scale = QK_DIM ** -0.5 * LOG2E
    cq = cq_ref[...]
    c, a, b = c_ref[...], a_ref[...], b_ref[...]
    for c0, r in _mm_chunks(cq, w_ref, 2 * QK_PAD):
        for hh in range(2):
            head = c0 // QK_PAD + hh
            nope = r[:, hh * QK_PAD:hh * QK_PAD + HEAD_DIM]
            rope = _rope128(r[:, hh * QK_PAD + HEAD_DIM:(hh + 1) * QK_PAD], c, a, b)
            q_ref[head, :, 0:HEAD_DIM] = (nope * scale).astype(BF16)
            q_ref[head, :, HEAD_DIM:QK_PAD] = (rope * scale).astype(BF16)


def _kvup_body(ckv_ref, kr_ref, w_ref, k_ref, v_ref):
    ckv = ckv_ref[...]
    kr = kr_ref[...]
    for c0, r in _mm_chunks(ckv, w_ref, 2 * QK_PAD):
        for hh in range(2):
            head = c0 // QK_PAD + hh
            k_ref[head, :, 0:HEAD_DIM] = r[:, hh * QK_PAD:hh * QK_PAD + HEAD_DIM].astype(BF16)
            k_ref[head, :, HEAD_DIM:QK_PAD] = kr
            v_ref[head, :, 0:HEAD_DIM] = r[:, hh * QK_PAD + HEAD_DIM:(hh + 1) * QK_PAD].astype(BF16)
            v_ref[head, :, HEAD_DIM:QK_PAD] = jnp.ones((r.shape[0], HEAD_DIM), BF16)


def _attn_body(q_ref, k_ref, v_ref, o_ref, m_ref, acc_ref, p_ref, alpha_ref, *, tq, tk):
    qi = pl.program_id(1)
    nh = q_ref.shape[0]
    m_ref[...] = jnp.full(m_ref.shape, NEG, F32)
    acc_ref[...] = jnp.zeros(acc_ref.shape, F32)
    kv_rows = lambda j: pl.ds(pl.multiple_of(j * tk, tk), tk)

    def probs(j, slot, masked):
        for hh in range(nh):
            s = lax.dot_general(q_ref[hh], k_ref[hh, kv_rows(j), :], (((1,), (1,)), ((), ())),
                                preferred_element_type=F32)
            if masked:
                row = lax.broadcasted_iota(jnp.int32, s.shape, 0)
                col = lax.broadcasted_iota(jnp.int32, s.shape, 1)
                s = jnp.where(col - row <= qi * tq - j * tk, s, NEG)
            m_old = m_ref[hh]
            m_new = jnp.maximum(m_old, jnp.max(s, axis=-1, keepdims=True))
            m_ref[hh] = m_new
            alpha_ref[slot, hh] = jnp.exp2(m_old - m_new)
            p_ref[slot, hh] = jnp.exp2(s - m_new).astype(BF16)

    def accumulate(j, slot):
        for hh in range(nh):
            acc_ref[hh] = (alpha_ref[slot, hh] * acc_ref[hh]
                           + jnp.dot(p_ref[slot, hh], v_ref[hh, kv_rows(j), :], preferred_element_type=F32))

    def step(j, slot, masked):
        probs(j + 1, 1 - slot, masked)
        accumulate(j, slot)

    n_last = (qi * tq) // tk
    n_plain = jnp.maximum(n_last - 1, 0)

    @pl.when(n_last >= 1)
    def _():
        probs(0, 0, False)

    @pl.when(n_last == 0)
    def _():
        probs(0, 0, True)

    def pair(i, carry):
        step(2 * i, 0, False)
        step(2 * i + 1, 1, False)
        return carry

    lax.fori_loop(0, n_plain // 2, pair, 0)

    @pl.when(n_plain % 2 == 1)
    def _():
        step(n_plain - 1, 0, False)

    for parity in (0, 1):
        @pl.when((n_last >= 1) & ((n_last - 1) % 2 == parity))
        def _():
            step(n_last - 1, parity, True)

    for parity in (0, 1):
        @pl.when(n_last % 2 == parity)
        def _():
            accumulate(n_last, parity)
    for hh in range(nh):
        acc = acc_ref[hh]
        o_ref[hh] = (acc[:, :HEAD_DIM] / acc[:, HEAD_DIM:]).astype(BF16)


def _attention(q, k, v):
    h, s, _ = q.shape
    tq = min(ATTN_TILE, s)
    tk = min(ATTN_KV_TILE, s)
    nh = ATTN_HEADS
    assert tk % tq == 0
    kv_bytes = nh * 2 * _nbytes((s, QK_PAD), BF16)
    scratch = [pltpu.VMEM((nh, tq, 1), F32), pltpu.VMEM((nh, tq, QK_PAD), F32),
               pltpu.VMEM((2, nh, tq, tk), BF16), pltpu.VMEM((2, nh, tq, 1), F32)]
    scratch_bytes = nh * (3 * _nbytes((tq, V7X_LANES), F32) + _nbytes((tq, QK_PAD), F32)
                          + 2 * _nbytes((tq, tk), BF16))
    return pl.pallas_call(
        functools.partial(_attn_body, tq=tq, tk=tk),
        grid=(h // nh, s // tq),
        in_specs=[pl.BlockSpec((nh, tq, QK_PAD), lambda hh, i: (hh, i, 0)),
                  pl.BlockSpec((nh, s, QK_PAD), lambda hh, i: (hh, 0, 0), pipeline_mode=pl.Buffered(1)),
                  pl.BlockSpec((nh, s, QK_PAD), lambda hh, i: (hh, 0, 0), pipeline_mode=pl.Buffered(1))],
        out_specs=pl.BlockSpec((nh, tq, HEAD_DIM), lambda hh, i: (hh, i, 0)),
        out_shape=_sds((h, s, HEAD_DIM), BF16),
        scratch_shapes=scratch,
        compiler_params=pltpu.CompilerParams(
            dimension_semantics=("arbitrary", "arbitrary"),
            vmem_limit_bytes=_vmem_limit(kv_bytes, 4 * nh * _nbytes((tq, QK_PAD), BF16), scratch_bytes,
                                         3 * nh * _nbytes((tq, tk), F32))),
        name="attn",
    )(q, k, v)


def _hgrn_chunk(qf, kf, vb, b, st, c):
    nt = (((1,), (1,)), ((), ()))

    o = lax.dot_general((qf * jnp.exp2(b)).astype(BF16), st.astype(BF16), nt, preferred_element_type=F32)

    row = lax.broadcasted_iota(jnp.int32, (c, c), 0)
    col = lax.broadcasted_iota(jnp.int32, (c, c), 1)
    shp8 = (c // V7X_SUBLANES, V7X_SUBLANES, HEAD_DIM)
    pos8 = lax.broadcasted_iota(jnp.int32, shp8, 1)
    scores = None
    m = c // 2
    while m >= 1:
        if m >= 2:
            shp = (c // (2 * m), 2 * m, HEAD_DIM) if m >= V7X_SUBLANES else shp8
            b3 = b.reshape(shp)
            if m >= V7X_SUBLANES:
                mid = b3[:, m - 1:m, :]
                second = lax.broadcasted_iota(jnp.int32, shp, 1) >= m
            elif m == 4:
                mid = b3[:, 3:4, :]
                second = pos8 >= 4
            else:
                mid = jnp.where(pos8 < 4, b3[:, 1:2, :], b3[:, 5:6, :])
                second = (pos8 & 3) >= 2
            d = b3 - mid
            qt = (jnp.exp2(jnp.where(second, d, NEG)) * qf.reshape(shp)).reshape(c, HEAD_DIM)
            kt = (jnp.exp2(jnp.where(second, NEG, -d)) * kf.reshape(shp)).reshape(c, HEAD_DIM)
        else:
            odd = (lax.broadcasted_iota(jnp.int32, b.shape, 0) & 1) == 1
            qt = jnp.exp2(jnp.where(odd, b - pltpu.roll(b, 1, 0), NEG)) * qf
            kt = jnp.where(odd, 0.0, kf)
        s_l = lax.dot_general(qt.astype(BF16), kt.astype(BF16), nt, preferred_element_type=F32)
        if scores is None:
            scores = s_l
        else:
            shift = (2 * m).bit_length() - 1
            scores = jnp.where((row >> shift) == (col >> shift), s_l, scores)
        m //= 2
    scores = jnp.where(row == col, jnp.sum(qf * kf, axis=-1, keepdims=True), scores)
    o = o + jnp.dot(scores.astype(BF16), vb, preferred_element_type=F32)

    b_last = b[c - 1:c, :]
    kd = (kf * jnp.exp2(b_last - b)).astype(BF16)
    st_new = st * jnp.exp2(b_last) + lax.dot_general(vb, kd, (((0,), (0,)), ((), ())),
                                                    preferred_element_type=F32)
    return o, st_new


def _hgrn_body(q_ref, b_ref, k_ref, v_ref, og_ref, gain_ref, o_ref, st_ref, *, c, rows):
    @pl.when(pl.program_id(1) == 0)
    def _():
        st_ref[...] = jnp.zeros_like(st_ref)

    gain = gain_ref[...]

    def step(i, carry):
        sl = pl.ds(pl.multiple_of(i * c, c), c)
        for hh in range(q_ref.shape[0]):
            o, st_new = _hgrn_chunk(q_ref[hh, sl, :].astype(F32), k_ref[hh, sl, :].astype(F32),
                                    v_ref[hh, sl, :], b_ref[hh, sl, :], st_ref[hh], c)
            st_ref[hh] = st_new
            o_ref[hh, sl, :] = (_rms(o, gain) * og_ref[hh, sl, :].astype(F32)).astype(BF16)
        return carry

    lax.fori_loop(0, rows // c, step, 0)


def _hgrn(q, b, k, v, og, gain):
    h, s, d = q.shape
    rows = min(HGRN_ROWS, s)
    c = HGRN_CHUNK
    hb = HGRN_HEADS
    spec = pl.BlockSpec((hb, rows, d), lambda hh, i: (hh, i, 0))
    return pl.pallas_call(
        functools.partial(_hgrn_body, c=c, rows=rows),
        grid=(h // hb, s // rows),
        in_specs=[spec, spec, spec, spec, spec, pl.BlockSpec((1, d), lambda hh, i: (0, 0))],
        out_specs=spec,
        out_shape=_sds((h, s, d), BF16),
        scratch_shapes=[pltpu.VMEM((hb, d, d), F32)],
        compiler_params=pltpu.CompilerParams(
            dimension_semantics=("arbitrary", "arbitrary"),
            vmem_limit_bytes=_vmem_limit(2 * 5 * hb * _nbytes((rows, d), F32), 64 * hb * _nbytes((c, d), F32))),
        name="hgrn",
    )(q, b, k, v, og, gain)


def _cat_heads(ref):
    return jnp.concatenate([ref[hh] for hh in range(ref.shape[0])], axis=-1)


def _mix_body(oa_ref, ob_ref, sa_ref, sb_ref, x_ref,
              woa_ref, wob_ref, wout_ref, gpost_ref, gate_ref, gpre_ref, sc_ref, sh_ref,
              x1_ref, h2_ref):
    ya = jnp.dot(_cat_heads(oa_ref), woa_ref[...], preferred_element_type=F32)
    yb = jnp.dot(_cat_heads(ob_ref), wob_ref[...], preferred_element_type=F32)
    merged = sa_ref[...].astype(F32) * ya + sb_ref[...].astype(F32) * yb
    y = jnp.dot(merged.astype(BF16), wout_ref[...], preferred_element_type=F32)
    x1 = x_ref[...] + gate_ref[...] * _rms(y, gpost_ref[...])
    x1_ref[...] = x1
    h2_ref[...] = (_rms(x1, gpre_ref[...]) * (1.0 + sc_ref[...]) + sh_ref[...]).astype(BF16)


def _gateup_body(h_ref, wg_ref, wu_ref, o_ref):
    h = h_ref[...]
    gte = jnp.dot(h, wg_ref[...].astype(BF16), preferred_element_type=F32)
    up = jnp.dot(h, wu_ref[...].astype(BF16), preferred_element_type=F32)
    o_ref[...] = (gte * jax.nn.sigmoid(gte) * up).astype(BF16)


def _gateup(h2, w_gu, d_ff):
    s, d = h2.shape
    tm = min(1024, s)
    tn = 512
    nj = d_ff // tn
    return pl.pallas_call(
        _gateup_body,
        grid=(s // tm, nj),
        in_specs=[pl.BlockSpec((tm, d), lambda i, j: (i, 0)),
                  pl.BlockSpec((None, d, tn), lambda i, j: (0, 0, j)),
                  pl.BlockSpec((None, d, tn), lambda i, j: (0, 0, j + nj))],
        out_specs=pl.BlockSpec((tm, tn), lambda i, j: (i, j)),
        out_shape=_sds((s, d_ff), BF16),
        compiler_params=pltpu.CompilerParams(
            dimension_semantics=("arbitrary", "arbitrary"),
            vmem_limit_bytes=_vmem_limit(2 * _nbytes((tm, d), BF16), 4 * _nbytes((d, tn), F32),
                                         2 * _nbytes((d, tn), BF16),
                                         2 * _nbytes((tm, tn), BF16), 4 * _nbytes((tm, tn), F32))),
        name="gateup",
    )(h2, w_gu, w_gu)


def _down_body(a_ref, x_ref, w_ref, gpost_ref, gate_ref, o_ref):
    y = jnp.dot(a_ref[...], w_ref[...], preferred_element_type=F32)
    o_ref[...] = x_ref[...] + gate_ref[...] * _rms(y, gpost_ref[...])


def kernel(x, c, positions, w_ada, b_ada, g_pre_mix, w_in, lb_logits, g_hg_out, w_o_hg, g_q_lora, w_uq,
           g_kv_lora, w_ukv, w_o_mla, w_out, g_post_mix, g_pre_ffn, w_gate_up, w_down, g_post_ffn):
    bsz, s, d = x.shape
    assert bsz == 1 and d == N_HEADS * HEAD_DIM
    d_ff = w_down.shape[1]
    x2 = x.reshape(s, d)
    row = lambda v: v.reshape(1, -1)
    hm = lambda dt: _sds((N_HEADS, s, HEAD_DIM), dt)

    w_in_t = jnp.swapaxes(w_in, 1, 2)
    o_cq, o_kr = 4 * d, 4 * d + 2 * LORA
    o_ga = o_kr + ROPE_DIM
    o_gb = o_ga + d
    w_uq_p = jnp.pad(w_uq[0].reshape(LORA, N_HEADS, QK_DIM),
                     ((0, 0), (0, 0), (0, QK_PAD - QK_DIM))).reshape(LORA, N_HEADS * QK_PAD).astype(BF16)
    w_ukv_b = w_ukv[0].astype(BF16)
    inv_freq = ROPE_THETA ** (-jnp.arange(0, ROPE_DIM, 2, dtype=F32) / ROPE_DIM)
    inv_row = jnp.tile(inv_freq, V7X_LANES // ROPE_HALF).reshape(1, V7X_LANES)

    mod = _ada(c.reshape(d, 1), w_ada[0], row(b_ada[0]))
    sh_m, sc_m, ga_m, sh_f, sc_f, ga_f = (mod[:, i * d:(i + 1) * d] for i in range(6))

    tm = min(1024, s)
    tab = _sds((s, V7X_LANES), F32)
    rc, ra, rb = _row_call(_ropetab_body, [positions.reshape(s, 1)], [inv_row], [tab, tab, tab],
                           tm, 8 * _nbytes((tm, V7X_LANES), F32), "ropetab")

    tm = min(512, s)
    (h,) = _row_call(_prenorm_body, [x2], [row(g_pre_mix[0]), sc_m, sh_m], [_sds((s, d), BF16)],
                     tm, 3 * _nbytes((tm, d), F32), "prenorm")

    tmp = 6 * _nbytes((tm, 512), F32)
    wsc = lambda n: [((n, d), BF16)]
    wt = lambda r0, n: (w_in_t, n, r0)
    (q_h,) = _row_call(_in_silu_body, [h], [wt(0, d)], [hm(BF16)], tm, tmp, "in_q", wsc(d))
    b_h, k_h = _row_call(_in_forget_body, [h], [wt(d, d), lb_logits], [hm(F32), hm(BF16)], tm, tmp, "in_f",
                         wsc(d))
    (v_h,) = _row_call(_in_plain_body, [h], [wt(2 * d, d)], [hm(BF16)], tm, tmp, "in_v", wsc(d))
    (og_h,) = _row_call(_in_silu_body, [h], [wt(3 * d, d)], [hm(BF16)], tm, tmp, "in_og", wsc(d))
    (cq_n,) = _row_call(_in_lora_body, [h], [wt(o_cq, LORA), row(g_q_lora[0])],
                        [_sds((s, LORA), BF16)], tm, tmp, "in_cq", wsc(LORA))
    (ckv_n,) = _row_call(_in_lora_body, [h], [wt(o_cq + LORA, LORA), row(g_kv_lora[0])],
                         [_sds((s, LORA), BF16)], tm, tmp, "in_ckv", wsc(LORA))
    (k_rope,) = _row_call(_in_krope_body, [h, rc, ra, rb], [wt(o_kr, V7X_LANES)],
                          [_sds((s, V7X_LANES), BF16)], tm, tmp, "in_kr", wsc(V7X_LANES))
    (sig_a,) = _row_call(_in_gate_body, [h], [wt(o_ga, d)], [_sds((s, d), BF16)], tm, tmp, "in_ga", wsc(d))
    (sig_b,) = _row_call(_in_gate_body, [h], [wt(o_gb, d)], [_sds((s, d), BF16)], tm, tmp, "in_gb", wsc(d))

    o_a = _hgrn(q_h, b_h, k_h, v_h, og_h, row(g_hg_out[0]))

    (q_m,) = _row_call(_qup_body, [cq_n, rc, ra, rb], [w_uq_p], [_sds((N_HEADS, s, QK_PAD), BF16)],
                       tm, tmp, "qup")
    k_m, v_m = _row_call(_kvup_body, [ckv_n, k_rope], [w_ukv_b],
                         [_sds((N_HEADS, s, QK_PAD), BF16)] * 2, tm, tmp, "kvup")
    o_b = _attention(q_m, k_m, v_m)

    tm = min(256, s)
    x1, h2 = _row_call(
        _mix_body, [o_a, o_b, sig_a, sig_b, x2],
        [w_o_hg[0].astype(BF16), w_o_mla[0].astype(BF16), w_out[0].astype(BF16),
         row(g_post_mix[0]), ga_m, row(g_pre_ffn[0]), sc_f, sh_f],
        [_sds((s, d), F32), _sds((s, d), BF16)], tm, 8 * _nbytes((tm, d), F32), "mix")

    act = _gateup(h2, w_gate_up, d_ff)
    (out,) = _row_call(_down_body, [act, x1], [w_down[0].astype(BF16), row(g_post_ffn[0]), ga_f],
                       [_sds((s, d), F32)], tm, 4 * _nbytes((tm, d), F32), "down")
    return out.reshape(bsz, s, d)
```

```python
import functools

import jax
import jax.numpy as jnp
from jax import lax
from jax.experimental import pallas as pl
from jax.experimental.pallas import tpu as pltpu

F32 = jnp.float32
BF16 = jnp.bfloat16

V7X_VMEM_BYTES = 64 * 1024 * 1024
V7X_LANES = 128
V7X_SUBLANES = 8

EPS = 1e-6
N_HEADS = 16
HEAD_DIM = 128
ROPE_DIM = 64
ROPE_HALF = ROPE_DIM // 2
QK_DIM = HEAD_DIM + ROPE_DIM
QK_PAD = 2 * HEAD_DIM
LORA = 512
ROPE_THETA = 10000.0
NEG = -1e30
LOG2E = 1.4426950408889634

HGRN_CHUNK = 128
HGRN_ROWS = 1024
HGRN_HEADS = 8
ATTN_TILE = 512
ATTN_KV_TILE = 1024
ATTN_HEADS = 4


def _vmem_limit(*nbytes):
    need = int(sum(nbytes))
    return int(min(V7X_VMEM_BYTES - (4 << 20), need + (8 << 20)))


def _nbytes(shape, dtype):
    n = 1
    for s in shape:
        n *= s
    return n * jnp.dtype(dtype).itemsize


def _row_spec(arr_shape, tm):
    if len(arr_shape) == 2:
        return pl.BlockSpec((tm, arr_shape[1]), lambda i: (i, 0))
    return pl.BlockSpec((arr_shape[0], tm, arr_shape[2]), lambda i: (0, i, 0))


def _const_spec(arr_shape):
    nd = len(arr_shape)
    return pl.BlockSpec(tuple(arr_shape), lambda i: (0,) * nd)


def _row_call(body, row_ins, const_ins, outs, tm, temp_bytes, name, scratch=()):
    m = row_ins[0].shape[-2]
    assert m % tm == 0, (m, tm)
    in_specs = [_row_spec(a.shape, tm) for a in row_ins]
    const_bytes = 0
    const_args = []
    for cin in const_ins:
        if isinstance(cin, tuple):
            arr, height, r0 = cin
            in_specs.append(pl.BlockSpec((None, pl.Element(height), pl.Element(arr.shape[2])),
                                         lambda i, r0=r0: (0, r0, 0), pipeline_mode=pl.Buffered(1)))
            const_bytes += _nbytes((height, arr.shape[2]), arr.dtype)
        else:
            arr = cin
            in_specs.append(_const_spec(arr.shape))
            const_bytes += _nbytes(arr.shape, arr.dtype)
        const_args.append(arr)
    out_specs = [_row_spec(o.shape, tm) for o in outs]
    tile = lambda s: tuple(s[:-2]) + (tm, s[-1])
    need = (2 * sum(_nbytes(tile(a.shape), a.dtype) for a in row_ins)
            + 2 * sum(_nbytes(tile(o.shape), o.dtype) for o in outs)
            + const_bytes + sum(_nbytes(shp, dt) for shp, dt in scratch))
    res = pl.pallas_call(
        body,
        grid=(m // tm,),
        in_specs=in_specs,
        out_specs=out_specs,
        out_shape=outs,
        scratch_shapes=[pltpu.VMEM(shp, dt) for shp, dt in scratch],
        compiler_params=pltpu.CompilerParams(
            dimension_semantics=("arbitrary",),
            vmem_limit_bytes=_vmem_limit(need, temp_bytes)),
        name=name,
    )(*row_ins, *const_args)
    return res


def _weights_bf16(w_ref, wb_ref):
    @pl.when(pl.program_id(0) == 0)
    def _():
        step = min(256, w_ref.shape[0])
        for r0 in range(0, w_ref.shape[0], step):
            wb_ref[r0:r0 + step, :] = w_ref[r0:r0 + step, :].astype(BF16)


def _mm_chunks_t(a, wt_ref, nc):
    for c0 in range(0, wt_ref.shape[0], nc):
        yield c0, lax.dot_general(a, wt_ref[c0:c0 + nc, :], (((1,), (1,)), ((), ())),
                                  preferred_element_type=F32)


def _sds(shape, dtype):
    return jax.ShapeDtypeStruct(tuple(shape), dtype)


def _rms(y, gain):
    return y * lax.rsqrt(jnp.mean(y * y, axis=-1, keepdims=True) + EPS) * gain


def _rope128(x, c, a, b):
    return x * c + pltpu.roll(x, 96, 1) * a + pltpu.roll(x, 32, 1) * b


def _ada_body(c_ref, w_ref, b_ref, o_ref):
    cc = c_ref[...]
    sc = cc * jax.nn.sigmoid(cc)
    o_ref[...] = jnp.sum(sc * w_ref[...], axis=0, keepdims=True) + b_ref[...]


def _ada(c_col, w, b_row):
    d, n = w.shape
    tn = 1024
    return pl.pallas_call(
        _ada_body,
        grid=(n // tn,),
        in_specs=[pl.BlockSpec((d, 1), lambda j: (0, 0)),
                  pl.BlockSpec((d, tn), lambda j: (0, j)),
                  pl.BlockSpec((1, tn), lambda j: (0, j))],
        out_specs=pl.BlockSpec((1, tn), lambda j: (0, j)),
        out_shape=_sds((1, n), F32),
        compiler_params=pltpu.CompilerParams(
            dimension_semantics=("arbitrary",),
            vmem_limit_bytes=_vmem_limit(3 * _nbytes((d, tn), F32), _nbytes((d, V7X_LANES), F32))),
        name="ada",
    )(c_col, w, b_row)


def _ropetab_body(pos_ref, inv_ref, c_ref, a_ref, b_ref):
    ang = pos_ref[...].astype(F32) * inv_ref[...]
    cos = jnp.cos(ang)
    sin = jnp.sin(ang)
    lane = lax.broadcasted_iota(jnp.int32, ang.shape, 1)
    c_ref[...] = jnp.where(lane < ROPE_DIM, cos, 0.0)
    a_ref[...] = jnp.where(lane < ROPE_HALF, -sin, 0.0)
    b_ref[...] = jnp.where((lane >= ROPE_HALF) & (lane < ROPE_DIM), sin, 0.0)


def _prenorm_body(x_ref, g_ref, sc_ref, sh_ref, h_ref):
    x = x_ref[...]
    h_ref[...] = (_rms(x, g_ref[...]) * (1.0 + sc_ref[...]) + sh_ref[...]).astype(BF16)


def _heads_out(o_ref, c0, val):
    for hh in range(val.shape[1] // HEAD_DIM):
        o_ref[c0 // HEAD_DIM + hh] = val[:, hh * HEAD_DIM:(hh + 1) * HEAD_DIM]


def _mm_chunks(a, w_ref, nc):
    n = w_ref.shape[1]
    for c0 in range(0, n, nc):
        yield c0, jnp.dot(a, w_ref[:, c0:c0 + nc], preferred_element_type=F32)


def _in_silu_body(h_ref, w_ref, o_ref, wb_ref):
    _weights_bf16(w_ref, wb_ref)
    for c0, r in _mm_chunks_t(h_ref[...], wb_ref, 512):
        _heads_out(o_ref, c0, (r * jax.nn.sigmoid(r)).astype(BF16))


def _in_plain_body(h_ref, w_ref, o_ref, wb_ref):
    _weights_bf16(w_ref, wb_ref)
    for c0, r in _mm_chunks_t(h_ref[...], wb_ref, 512):
        _heads_out(o_ref, c0, r.astype(BF16))


def _chunk_cumsum(g, c):
    rows, width = g.shape
    sub = V7X_SUBLANES
    x = g.reshape(rows // sub, sub, width)
    pos = lax.broadcasted_iota(jnp.int32, x.shape, 1)
    sh = 1
    while sh < sub:
        x = x + jnp.where(pos >= sh, pltpu.roll(x, sh, 1), 0.0)
        sh *= 2
    x = x.reshape(rows // c, c // sub, sub, width)
    run = None
    out = []
    for k in range(c // sub):
        blk = x[:, k]
        blk = blk if run is None else blk + run
        out.append(blk)
        run = blk[:, sub - 1:sub, :]
    return jnp.stack(out, axis=1).reshape(rows, width)


def _in_forget_body(h_ref, w_ref, lb_ref, b_ref, k_ref, wb_ref):
    _weights_bf16(w_ref, wb_ref)
    logits = lb_ref[...]
    e = jnp.exp(logits - jnp.max(logits, axis=0, keepdims=True))
    lb_all = e[0:1, :] / jnp.sum(e, axis=0, keepdims=True)
    for c0, r in _mm_chunks_t(h_ref[...], wb_ref, 512):
        lb = lb_all[:, c0:c0 + 512]
        f = lb + (1.0 - lb) * jax.nn.sigmoid(r)
        _heads_out(b_ref, c0, _chunk_cumsum(jnp.log2(f), HGRN_CHUNK))
        _heads_out(k_ref, c0, (1.0 - f).astype(BF16))


def _in_lora_body(h_ref, w_ref, gain_ref, o_ref, wb_ref):
    _weights_bf16(w_ref, wb_ref)
    ((_, r),) = _mm_chunks_t(h_ref[...], wb_ref, wb_ref.shape[0])
    o_ref[...] = _rms(r, gain_ref[...]).astype(BF16)


def _in_krope_body(h_ref, c_ref, a_ref, b_ref, w_ref, o_ref, wb_ref):
    _weights_bf16(w_ref, wb_ref)
    ((_, r),) = _mm_chunks_t(h_ref[...], wb_ref, wb_ref.shape[0])
    o_ref[...] = _rope128(r, c_ref[...], a_ref[...], b_ref[...]).astype(BF16)


def _in_gate_body(h_ref, w_ref, o_ref, wb_ref):
    _weights_bf16(w_ref, wb_ref)
    for c0, r in _mm_chunks_t(h_ref[...], wb_ref, 512):
        o_ref[:, c0:c0 + 512] = jax.nn.sigmoid(r).astype(BF16)


def _qup_body(cq_ref, c_ref, a_ref, b_ref, w_ref, q_ref):
    scale = QK_DIM ** -0.5 * LOG2E
    cq = cq_ref[...]
    c, a, b = c_ref[...], a_ref[...], b_ref[...]
    for c0, r in _mm_chunks(cq, w_ref, 2 * QK_PAD):
        for hh in range(2):
            head = c0 // QK_PAD + hh
            nope = r[:, hh * QK_PAD:hh * QK_PAD + HEAD_DIM]
            rope = _rope128(r[:, hh * QK_PAD + HEAD_DIM:(hh + 1) * QK_PAD], c, a, b)
            q_ref[head, :, 0:HEAD_DIM] = (nope * scale).astype(BF16)
            q_ref[head, :, HEAD_DIM:QK_PAD] = (rope * scale).astype(BF16)


def _kvup_body(ckv_ref, kr_ref, w_ref, k_ref, v_ref):
    ckv = ckv_ref[...]
    kr = kr_ref[...]
    for c0, r in _mm_chunks(ckv, w_ref, 2 * QK_PAD):
        for hh in range(2):
            head = c0 // QK_PAD + hh
            k_ref[head, :, 0:HEAD_DIM] = r[:, hh * QK_PAD:hh * QK_PAD + HEAD_DIM].astype(BF16)
            k_ref[head, :, HEAD_DIM:QK_PAD] = kr
            v_ref[head] = r[:, hh * QK_PAD + HEAD_DIM:(hh + 1) * QK_PAD].astype(BF16)


def _attn_body(q_ref, k_ref, v_ref, o_ref, m_ref, acc_ref, p_ref, alpha_ref, *, tq, tk):
    qi = pl.program_id(1)
    nh = q_ref.shape[0]
    m_ref[...] = jnp.full(m_ref.shape, NEG, F32)
    acc_ref[...] = jnp.zeros(acc_ref.shape, F32)
    kv_rows = lambda j: pl.ds(pl.multiple_of(j * tk, tk), tk)

    def probs(j, slot, masked):
        for hh in range(nh):
            s = lax.dot_general(q_ref[hh], k_ref[hh, kv_rows(j), :], (((1,), (1,)), ((), ())),
                                preferred_element_type=F32)
            if masked:
                row = lax.broadcasted_iota(jnp.int32, s.shape, 0)
                col = lax.broadcasted_iota(jnp.int32, s.shape, 1)
                s = jnp.where(col - row <= qi * tq - j * tk, s, NEG)
            m_old = m_ref[hh]
            m_new = jnp.maximum(m_old, jnp.max(s, axis=-1, keepdims=True))
            m_ref[hh] = m_new
            alpha_ref[slot, hh] = jnp.exp2(m_old - m_new)
            p_ref[slot, hh] = jnp.exp2(s - m_new).astype(BF16)

    def accumulate(j, slot):
        ones = jnp.ones((tk, HEAD_DIM), BF16)
        for hh in range(nh):
            v_ext = jnp.concatenate([v_ref[hh, kv_rows(j), :], ones], axis=-1)
            acc_ref[hh] = (alpha_ref[slot, hh] * acc_ref[hh]
                           + jnp.dot(p_ref[slot, hh], v_ext, preferred_element_type=F32))

    def step(j, slot, masked):
        probs(j + 1, 1 - slot, masked)
        accumulate(j, slot)

    n_last = (qi * tq) // tk
    n_plain = jnp.maximum(n_last - 1, 0)

    @pl.when(n_last >= 1)
    def _():
        probs(0, 0, False)

    @pl.when(n_last == 0)
    def _():
        probs(0, 0, True)

    def pair(i, carry):
        step(2 * i, 0, False)
        step(2 * i + 1, 1, False)
        return carry

    lax.fori_loop(0, n_plain // 2, pair, 0)

    @pl.when(n_plain % 2 == 1)
    def _():
        step(n_plain - 1, 0, False)

    for parity in (0, 1):
        @pl.when((n_last >= 1) & ((n_last - 1) % 2 == parity))
        def _():
            step(n_last - 1, parity, True)

    for parity in (0, 1):
        @pl.when(n_last % 2 == parity)
        def _():
            accumulate(n_last, parity)
    for hh in range(nh):
        acc = acc_ref[hh]
        o_ref[hh] = (acc[:, :HEAD_DIM] / acc[:, HEAD_DIM:]).astype(BF16)


def _attention(q, k, v):
    h, s, _ = q.shape
    tq = min(ATTN_TILE, s)
    tk = min(ATTN_KV_TILE, s)
    nh = ATTN_HEADS
    assert tk % tq == 0
    kv_bytes = nh * (_nbytes((s, QK_PAD), BF16) + _nbytes((s, HEAD_DIM), BF16))
    scratch = [pltpu.VMEM((nh, tq, 1), F32), pltpu.VMEM((nh, tq, QK_PAD), F32),
               pltpu.VMEM((2, nh, tq, tk), BF16), pltpu.VMEM((2, nh, tq, 1), F32)]
    scratch_bytes = nh * (3 * _nbytes((tq, V7X_LANES), F32) + _nbytes((tq, QK_PAD), F32)
                          + 2 * _nbytes((tq, tk), BF16))
    return pl.pallas_call(
        functools.partial(_attn_body, tq=tq, tk=tk),
        grid=(h // nh, s // tq),
        in_specs=[pl.BlockSpec((nh, tq, QK_PAD), lambda hh, i: (hh, i, 0)),
                  pl.BlockSpec((nh, s, QK_PAD), lambda hh, i: (hh, 0, 0), pipeline_mode=pl.Buffered(1)),
                  pl.BlockSpec((nh, s, HEAD_DIM), lambda hh, i: (hh, 0, 0), pipeline_mode=pl.Buffered(1))],
        out_specs=pl.BlockSpec((nh, tq, HEAD_DIM), lambda hh, i: (hh, i, 0)),
        out_shape=_sds((h, s, HEAD_DIM), BF16),
        scratch_shapes=scratch,
        compiler_params=pltpu.CompilerParams(
            dimension_semantics=("arbitrary", "arbitrary"),
            vmem_limit_bytes=_vmem_limit(kv_bytes, 4 * nh * _nbytes((tq, QK_PAD), BF16), scratch_bytes,
                                         3 * nh * _nbytes((tq, tk), F32))),
        name="attn",
    )(q, k, v)


def _hgrn_chunk(qf, kf, vb, b, st, c):
    nt = (((1,), (1,)), ((), ()))

    o = lax.dot_general((qf * jnp.exp2(b)).astype(BF16), st.astype(BF16), nt, preferred_element_type=F32)

    row = lax.broadcasted_iota(jnp.int32, (c, c), 0)
    col = lax.broadcasted_iota(jnp.int32, (c, c), 1)
    shp8 = (c // V7X_SUBLANES, V7X_SUBLANES, HEAD_DIM)
    pos8 = lax.broadcasted_iota(jnp.int32, shp8, 1)
    scores = None
    m = c // 2
    while m >= 1:
        if m >= 2:
            shp = (c // (2 * m), 2 * m, HEAD_DIM) if m >= V7X_SUBLANES else shp8
            b3 = b.reshape(shp)
            if m >= V7X_SUBLANES:
                mid = b3[:, m - 1:m, :]
                second = lax.broadcasted_iota(jnp.int32, shp, 1) >= m
            elif m == 4:
                mid = b3[:, 3:4, :]
                second = pos8 >= 4
            else:
                mid = jnp.where(pos8 < 4, b3[:, 1:2, :], b3[:, 5:6, :])
                second = (pos8 & 3) >= 2
            d = b3 - mid
            qt = (jnp.exp2(jnp.where(second, d, NEG)) * qf.reshape(shp)).reshape(c, HEAD_DIM)
            kt = (jnp.exp2(jnp.where(second, NEG, -d)) * kf.reshape(shp)).reshape(c, HEAD_DIM)
        else:
            odd = (lax.broadcasted_iota(jnp.int32, b.shape, 0) & 1) == 1
            qt = jnp.exp2(jnp.where(odd, b - pltpu.roll(b, 1, 0), NEG)) * qf
            kt = jnp.where(odd, 0.0, kf)
        s_l = lax.dot_general(qt.astype(BF16), kt.astype(BF16), nt, preferred_element_type=F32)
        if scores is None:
            scores = s_l
        else:
            shift = (2 * m).bit_length() - 1
            scores = jnp.where((row >> shift) == (col >> shift), s_l, scores)
        m //= 2
    scores = jnp.where(row == col, jnp.sum(qf * kf, axis=-1, keepdims=True), scores)
    o = o + jnp.dot(scores.astype(BF16), vb, preferred_element_type=F32)

    b_last = b[c - 1:c, :]
    kd = (kf * jnp.exp2(b_last - b)).astype(BF16)
    st_new = st * jnp.exp2(b_last) + lax.dot_general(vb, kd, (((0,), (0,)), ((), ())),
                                                    preferred_element_type=F32)
    return o, st_new


def _hgrn_body(q_ref, b_ref, k_ref, v_ref, og_ref, gain_ref, o_ref, st_ref, *, c, rows):
    @pl.when(pl.program_id(1) == 0)
    def _():
        st_ref[...] = jnp.zeros_like(st_ref)

    gain = gain_ref[...]

    def step(i, carry):
        sl = pl.ds(pl.multiple_of(i * c, c), c)
        for hh in range(q_ref.shape[0]):
            o, st_new = _hgrn_chunk(q_ref[hh, sl, :].astype(F32), k_ref[hh, sl, :].astype(F32),
                                    v_ref[hh, sl, :], b_ref[hh, sl, :], st_ref[hh], c)
            st_ref[hh] = st_new
            o_ref[hh, sl, :] = (_rms(o, gain) * og_ref[hh, sl, :].astype(F32)).astype(BF16)
        return carry

    lax.fori_loop(0, rows // c, step, 0)


def _hgrn(q, b, k, v, og, gain):
    h, s, d = q.shape
    rows = min(HGRN_ROWS, s)
    c = HGRN_CHUNK
    hb = HGRN_HEADS
    spec = pl.BlockSpec((hb, rows, d), lambda hh, i: (hh, i, 0))
    return pl.pallas_call(
        functools.partial(_hgrn_body, c=c, rows=rows),
        grid=(h // hb, s // rows),
        in_specs=[spec, spec, spec, spec, spec, pl.BlockSpec((1, d), lambda hh, i: (0, 0))],
        out_specs=spec,
        out_shape=_sds((h, s, d), BF16),
        scratch_shapes=[pltpu.VMEM((hb, d, d), F32)],
        compiler_params=pltpu.CompilerParams(
            dimension_semantics=("arbitrary", "arbitrary"),
            vmem_limit_bytes=_vmem_limit(2 * 5 * hb * _nbytes((rows, d), F32), 64 * hb * _nbytes((c, d), F32))),
        name="hgrn",
    )(q, b, k, v, og, gain)


def _cat_heads(ref):
    return jnp.concatenate([ref[hh] for hh in range(ref.shape[0])], axis=-1)


def _mix_body(oa_ref, ob_ref, sa_ref, sb_ref, x_ref,
              woa_ref, wob_ref, wout_ref, gpost_ref, gate_ref, gpre_ref, sc_ref, sh_ref,
              x1_ref, h2_ref):
    ya = jnp.dot(_cat_heads(oa_ref), woa_ref[...], preferred_element_type=F32)
    yb = jnp.dot(_cat_heads(ob_ref), wob_ref[...], preferred_element_type=F32)
    merged = sa_ref[...].astype(F32) * ya + sb_ref[...].astype(F32) * yb
    y = jnp.dot(merged.astype(BF16), wout_ref[...], preferred_element_type=F32)
    x1 = x_ref[...] + gate_ref[...] * _rms(y, gpost_ref[...])
    x1_ref[...] = x1
    h2_ref[...] = (_rms(x1, gpre_ref[...]) * (1.0 + sc_ref[...]) + sh_ref[...]).astype(BF16)


def _gateup_body(h_ref, wg_ref, wu_ref, o_ref):
    h = h_ref[...]
    gte = jnp.dot(h, wg_ref[...].astype(BF16), preferred_element_type=F32)
    up = jnp.dot(h, wu_ref[...].astype(BF16), preferred_element_type=F32)
    o_ref[...] = (gte * jax.nn.sigmoid(gte) * up).astype(BF16)


def _gateup(h2, w_gu, d_ff):
    s, d = h2.shape
    tm = min(1024, s)
    tn = 512
    nj = d_ff // tn
    return pl.pallas_call(
        _gateup_body,
        grid=(s // tm, nj),
        in_specs=[pl.BlockSpec((tm, d), lambda i, j: (i, 0)),
                  pl.BlockSpec((None, d, tn), lambda i, j: (0, 0, j)),
                  pl.BlockSpec((None, d, tn), lambda i, j: (0, 0, j + nj))],
        out_specs=pl.BlockSpec((tm, tn), lambda i, j: (i, j)),
        out_shape=_sds((s, d_ff), BF16),
        compiler_params=pltpu.CompilerParams(
            dimension_semantics=("arbitrary", "arbitrary"),
            vmem_limit_bytes=_vmem_limit(2 * _nbytes((tm, d), BF16), 4 * _nbytes((d, tn), F32),
                                         2 * _nbytes((d, tn), BF16),
                                         2 * _nbytes((tm, tn), BF16), 4 * _nbytes((tm, tn), F32))),
        name="gateup",
    )(h2, w_gu, w_gu)


def _down_body(a_ref, x_ref, w_ref, gpost_ref, gate_ref, o_ref):
    y = jnp.dot(a_ref[...], w_ref[...], preferred_element_type=F32)
    o_ref[...] = x_ref[...] + gate_ref[...] * _rms(y, gpost_ref[...])


def kernel(x, c, positions, w_ada, b_ada, g_pre_mix, w_in, lb_logits, g_hg_out, w_o_hg, g_q_lora, w_uq,
           g_kv_lora, w_ukv, w_o_mla, w_out, g_post_mix, g_pre_ffn, w_gate_up, w_down, g_post_ffn):
    bsz, s, d = x.shape
    assert bsz == 1 and d == N_HEADS * HEAD_DIM
    d_ff = w_down.shape[1]
    x2 = x.reshape(s, d)
    row = lambda v: v.reshape(1, -1)
    hm = lambda dt: _sds((N_HEADS, s, HEAD_DIM), dt)

    w_in_t = jnp.swapaxes(w_in, 1, 2)
    o_cq, o_kr = 4 * d, 4 * d + 2 * LORA
    o_ga = o_kr + ROPE_DIM
    o_gb = o_ga + d
    w_uq_p = jnp.pad(w_uq[0].reshape(LORA, N_HEADS, QK_DIM),
                     ((0, 0), (0, 0), (0, QK_PAD - QK_DIM))).reshape(LORA, N_HEADS * QK_PAD).astype(BF16)
    w_ukv_b = w_ukv[0].astype(BF16)
    inv_freq = ROPE_THETA ** (-jnp.arange(0, ROPE_DIM, 2, dtype=F32) / ROPE_DIM)
    inv_row = jnp.tile(inv_freq, V7X_LANES // ROPE_HALF).reshape(1, V7X_LANES)

    mod = _ada(c.reshape(d, 1), w_ada[0], row(b_ada[0]))
    sh_m, sc_m, ga_m, sh_f, sc_f, ga_f = (mod[:, i * d:(i + 1) * d] for i in range(6))

    tm = min(1024, s)
    tab = _sds((s, V7X_LANES), F32)
    rc, ra, rb = _row_call(_ropetab_body, [positions.reshape(s, 1)], [inv_row], [tab, tab, tab],
                           tm, 8 * _nbytes((tm, V7X_LANES), F32), "ropetab")

    tm = min(512, s)
    (h,) = _row_call(_prenorm_body, [x2], [row(g_pre_mix[0]), sc_m, sh_m], [_sds((s, d), BF16)],
                     tm, 3 * _nbytes((tm, d), F32), "prenorm")

    tmp = 6 * _nbytes((tm, 512), F32)
    tm2 = min(1024, s)
    tmp2 = 4 * _nbytes((tm2, 512), F32)
    wsc = lambda n: [((n, d), BF16)]
    wt = lambda r0, n: (w_in_t, n, r0)
    (q_h,) = _row_call(_in_silu_body, [h], [wt(0, d)], [hm(BF16)], tm2, tmp2, "in_q", wsc(d))
    b_h, k_h = _row_call(_in_forget_body, [h], [wt(d, d), lb_logits], [hm(F32), hm(BF16)], tm, tmp, "in_f",
                         wsc(d))
    (v_h,) = _row_call(_in_plain_body, [h], [wt(2 * d, d)], [hm(BF16)], tm2, tmp2, "in_v", wsc(d))
    (og_h,) = _row_call(_in_silu_body, [h], [wt(3 * d, d)], [hm(BF16)], tm2, tmp2, "in_og", wsc(d))
    (cq_n,) = _row_call(_in_lora_body, [h], [wt(o_cq, LORA), row(g_q_lora[0])],
                        [_sds((s, LORA), BF16)], tm, tmp, "in_cq", wsc(LORA))
    (ckv_n,) = _row_call(_in_lora_body, [h], [wt(o_cq + LORA, LORA), row(g_kv_lora[0])],
                         [_sds((s, LORA), BF16)], tm, tmp, "in_ckv", wsc(LORA))
    (k_rope,) = _row_call(_in_krope_body, [h, rc, ra, rb], [wt(o_kr, V7X_LANES)],
                          [_sds((s, V7X_LANES), BF16)], tm, tmp, "in_kr", wsc(V7X_LANES))
    (sig_a,) = _row_call(_in_gate_body, [h], [wt(o_ga, d)], [_sds((s, d), BF16)], tm2, tmp2, "in_ga", wsc(d))
    (sig_b,) = _row_call(_in_gate_body, [h], [wt(o_gb, d)], [_sds((s, d), BF16)], tm2, tmp2, "in_gb", wsc(d))

    o_a = _hgrn(q_h, b_h, k_h, v_h, og_h, row(g_hg_out[0]))

    (q_m,) = _row_call(_qup_body, [cq_n, rc, ra, rb], [w_uq_p], [_sds((N_HEADS, s, QK_PAD), BF16)],
                       tm, tmp, "qup")
    k_m, v_m = _row_call(_kvup_body, [ckv_n, k_rope], [w_ukv_b],
                         [_sds((N_HEADS, s, QK_PAD), BF16), hm(BF16)], tm, tmp, "kvup")
    o_b = _attention(q_m, k_m, v_m)

    tm = min(256, s)
    x1, h2 = _row_call(
        _mix_body, [o_a, o_b, sig_a, sig_b, x2],
        [w_o_hg[0].astype(BF16), w_o_mla[0].astype(BF16), w_out[0].astype(BF16),
         row(g_post_mix[0]), ga_m, row(g_pre_ffn[0]), sc_f, sh_f],
        [_sds((s, d), F32), _sds((s, d), BF16)], tm, 8 * _nbytes((tm, d), F32), "mix")

    act = _gateup(h2, w_gate_up, d_ff)
    (out,) = _row_call(_down_body, [act, x1], [w_down[0].astype(BF16), row(g_post_ffn[0]), ga_f],
                       [_sds((s, d), F32)], tm, 4 * _nbytes((tm, d), F32), "down")
    return out.reshape(bsz, s, d)
```

```python
import functools

import jax
import jax.numpy as jnp
from jax import lax
from jax.experimental import pallas as pl
from jax.experimental.pallas import tpu as pltpu

F32 = jnp.float32
BF16 = jnp.bfloat16

V7X_VMEM_BYTES = 64 * 1024 * 1024
V7X_LANES = 128
V7X_SUBLANES = 8

EPS = 1e-6
N_HEADS = 16
HEAD_DIM = 128
ROPE_DIM = 64
ROPE_HALF = ROPE_DIM // 2
QK_DIM = HEAD_DIM + ROPE_DIM
QK_PAD = 2 * HEAD_DIM
LORA = 512
ROPE_THETA = 10000.0
NEG = -1e30
LOG2E = 1.4426950408889634

HGRN_CHUNK = 128
HGRN_ROWS = 1024
HGRN_HEADS = 8
ATTN_TILE = 512
ATTN_KV_TILE = 1024
ATTN_HEADS = 4


def _vmem_limit(*nbytes):
    need = int(sum(nbytes))
    return int(min(V7X_VMEM_BYTES - (4 << 20), need + (8 << 20)))


def _nbytes(shape, dtype):
    n = 1
    for s in shape:
        n *= s
    return n * jnp.dtype(dtype).itemsize


def _row_spec(arr_shape, tm):
    if len(arr_shape) == 2:
        return pl.BlockSpec((tm, arr_shape[1]), lambda i: (i, 0))
    return pl.BlockSpec((arr_shape[0], tm, arr_shape[2]), lambda i: (0, i, 0))


def _const_spec(arr_shape):
    nd = len(arr_shape)
    return pl.BlockSpec(tuple(arr_shape), lambda i: (0,) * nd)


def _row_call(body, row_ins, const_ins, outs, tm, temp_bytes, name, scratch=()):
    m = row_ins[0].shape[-2]
    assert m % tm == 0, (m, tm)
    in_specs = [_row_spec(a.shape, tm) for a in row_ins]
    const_bytes = 0
    const_args = []
    for cin in const_ins:
        if isinstance(cin, tuple):
            arr, height, r0 = cin
            in_specs.append(pl.BlockSpec((None, pl.Element(height), pl.Element(arr.shape[2])),
                                         lambda i, r0=r0: (0, r0, 0), pipeline_mode=pl.Buffered(1)))
            const_bytes += _nbytes((height, arr.shape[2]), arr.dtype)
        else:
            arr = cin
            in_specs.append(_const_spec(arr.shape))
            const_bytes += _nbytes(arr.shape, arr.dtype)
        const_args.append(arr)
    out_specs = [_row_spec(o.shape, tm) for o in outs]
    tile = lambda s: tuple(s[:-2]) + (tm, s[-1])
    need = (2 * sum(_nbytes(tile(a.shape), a.dtype) for a in row_ins)
            + 2 * sum(_nbytes(tile(o.shape), o.dtype) for o in outs)
            + const_bytes + sum(_nbytes(shp, dt) for shp, dt in scratch))
    res = pl.pallas_call(
        body,
        grid=(m // tm,),
        in_specs=in_specs,
        out_specs=out_specs,
        out_shape=outs,
        scratch_shapes=[pltpu.VMEM(shp, dt) for shp, dt in scratch],
        compiler_params=pltpu.CompilerParams(
            dimension_semantics=("arbitrary",),
            vmem_limit_bytes=_vmem_limit(need, temp_bytes)),
        name=name,
    )(*row_ins, *const_args)
    return res


def _weights_bf16(w_ref, wb_ref):
    @pl.when(pl.program_id(0) == 0)
    def _():
        step = min(256, w_ref.shape[0])
        for r0 in range(0, w_ref.shape[0], step):
            wb_ref[r0:r0 + step, :] = w_ref[r0:r0 + step, :].astype(BF16)


def _mm_chunks_t(a, wt_ref, nc):
    for c0 in range(0, wt_ref.shape[0], nc):
        yield c0, lax.dot_general(a, wt_ref[c0:c0 + nc, :], (((1,), (1,)), ((), ())),
                                  preferred_element_type=F32)


def _sds(shape, dtype):
    return jax.ShapeDtypeStruct(tuple(shape), dtype)


def _rms(y, gain):
    return y * lax.rsqrt(jnp.mean(y * y, axis=-1, keepdims=True) + EPS) * gain


def _rope128(x, c, a, b):
    return x * c + pltpu.roll(x, 96, 1) * a + pltpu.roll(x, 32, 1) * b


def _ada_body(c_ref, w_ref, b_ref, o_ref):
    cc = c_ref[...]
    sc = cc * jax.nn.sigmoid(cc)
    o_ref[...] = jnp.sum(sc * w_ref[...], axis=0, keepdims=True) + b_ref[...]


def _ada(c_col, w, b_row):
    d, n = w.shape
    tn = 1024
    return pl.pallas_call(
        _ada_body,
        grid=(n // tn,),
        in_specs=[pl.BlockSpec((d, 1), lambda j: (0, 0)),
                  pl.BlockSpec((d, tn), lambda j: (0, j)),
                  pl.BlockSpec((1, tn), lambda j: (0, j))],
        out_specs=pl.BlockSpec((1, tn), lambda j: (0, j)),
        out_shape=_sds((1, n), F32),
        compiler_params=pltpu.CompilerParams(
            dimension_semantics=("arbitrary",),
            vmem_limit_bytes=_vmem_limit(3 * _nbytes((d, tn), F32), _nbytes((d, V7X_LANES), F32))),
        name="ada",
    )(c_col, w, b_row)


def _ropetab_body(pos_ref, inv_ref, c_ref, a_ref, b_ref):
    ang = pos_ref[...].astype(F32) * inv_ref[...]
    cos = jnp.cos(ang)
    sin = jnp.sin(ang)
    lane = lax.broadcasted_iota(jnp.int32, ang.shape, 1)
    c_ref[...] = jnp.where(lane < ROPE_DIM, cos, 0.0)
    a_ref[...] = jnp.where(lane < ROPE_HALF, -sin, 0.0)
    b_ref[...] = jnp.where((lane >= ROPE_HALF) & (lane < ROPE_DIM), sin, 0.0)


def _prenorm_body(x_ref, g_ref, sc_ref, sh_ref, h_ref):
    x = x_ref[...]
    h_ref[...] = (_rms(x, g_ref[...]) * (1.0 + sc_ref[...]) + sh_ref[...]).astype(BF16)


def _heads_out(o_ref, c0, val):
    for hh in range(val.shape[1] // HEAD_DIM):
        o_ref[c0 // HEAD_DIM + hh] = val[:, hh * HEAD_DIM:(hh + 1) * HEAD_DIM]


def _mm_chunks(a, w_ref, nc):
    n = w_ref.shape[1]
    for c0 in range(0, n, nc):
        yield c0, jnp.dot(a, w_ref[:, c0:c0 + nc], preferred_element_type=F32)


def _in_silu_body(h_ref, w_ref, o_ref, wb_ref):
    _weights_bf16(w_ref, wb_ref)
    for c0, r in _mm_chunks_t(h_ref[...], wb_ref, 512):
        _heads_out(o_ref, c0, (r * jax.nn.sigmoid(r)).astype(BF16))


def _in_plain_body(h_ref, w_ref, o_ref, wb_ref):
    _weights_bf16(w_ref, wb_ref)
    for c0, r in _mm_chunks_t(h_ref[...], wb_ref, 512):
        _heads_out(o_ref, c0, r.astype(BF16))


def _chunk_cumsum(g, c):
    rows, width = g.shape
    sub = V7X_SUBLANES
    x = g.reshape(rows // sub, sub, width)
    pos = lax.broadcasted_iota(jnp.int32, x.shape, 1)
    sh = 1
    while sh < sub:
        x = x + jnp.where(pos >= sh, pltpu.roll(x, sh, 1), 0.0)
        sh *= 2
    x = x.reshape(rows // c, c // sub, sub, width)
    run = None
    out = []
    for k in range(c // sub):
        blk = x[:, k]
        blk = blk if run is None else blk + run
        out.append(blk)
        run = blk[:, sub - 1:sub, :]
    return jnp.stack(out, axis=1).reshape(rows, width)


def _in_forget_body(h_ref, w_ref, lb_ref, b_ref, k_ref, wb_ref):
    _weights_bf16(w_ref, wb_ref)
    logits = lb_ref[...]
    e = jnp.exp(logits - jnp.max(logits, axis=0, keepdims=True))
    lb_all = e[0:1, :] / jnp.sum(e, axis=0, keepdims=True)
    for c0, r in _mm_chunks_t(h_ref[...], wb_ref, 512):
        lb = lb_all[:, c0:c0 + 512]
        f = lb + (1.0 - lb) * jax.nn.sigmoid(r)
        _heads_out(b_ref, c0, _chunk_cumsum(jnp.log2(f), HGRN_CHUNK))
        _heads_out(k_ref, c0, (1.0 - f).astype(BF16))


def _in_lora_body(h_ref, w_ref, gain_ref, o_ref, wb_ref):
    _weights_bf16(w_ref, wb_ref)
    ((_, r),) = _mm_chunks_t(h_ref[...], wb_ref, wb_ref.shape[0])
    o_ref[...] = _rms(r, gain_ref[...]).astype(BF16)


def _in_krope_body(h_ref, c_ref, a_ref, b_ref, w_ref, o_ref, wb_ref):
    _weights_bf16(w_ref, wb_ref)
    ((_, r),) = _mm_chunks_t(h_ref[...], wb_ref, wb_ref.shape[0])
    o_ref[...] = _rope128(r, c_ref[...], a_ref[...], b_ref[...]).astype(BF16)


def _in_gate_body(h_ref, w_ref, o_ref, wb_ref):
    _weights_bf16(w_ref, wb_ref)
    for c0, r in _mm_chunks_t(h_ref[...], wb_ref, 512):
        o_ref[:, c0:c0 + 512] = jax.nn.sigmoid(r).astype(BF16)


def _qup_body(cq_ref, c_ref, a_ref, b_ref, w_ref, q_ref):
    scale = QK_DIM ** -0.5 * LOG2E
    cq = cq_ref[...]
    c, a, b = c_ref[...], a_ref[...], b_ref[...]
    for c0, r in _mm_chunks(cq, w_ref, 2 * QK_PAD):
        for hh in range(2):
            head = c0 // QK_PAD + hh
            nope = r[:, hh * QK_PAD:hh * QK_PAD + HEAD_DIM]
            rope = _rope128(r[:, hh * QK_PAD + HEAD_DIM:(hh + 1) * QK_PAD], c, a, b)
            q_ref[head, :, 0:HEAD_DIM] = (nope * scale).astype(BF16)
            q_ref[head, :, HEAD_DIM:QK_PAD] = (rope * scale).astype(BF16)


def _kvup_body(ckv_ref, kr_ref, w_ref, k_ref, v_ref):
    ckv = ckv_ref[...]
    kr = kr_ref[...]
    for c0, r in _mm_chunks(ckv, w_ref, 2 * QK_PAD):
        for hh in range(2):
            head = c0 // QK_PAD + hh
            k_ref[head, :, 0:HEAD_DIM] = r[:, hh * QK_PAD:hh * QK_PAD + HEAD_DIM].astype(BF16)
            k_ref[head, :, HEAD_DIM:QK_PAD] = kr
            v_ref[head] = r[:, hh * QK_PAD + HEAD_DIM:(hh + 1) * QK_PAD].astype(BF16)


def _attn_body(q_ref, k_ref, v_ref, o_ref, m_ref, acc_ref, p_ref, alpha_ref, *, tq, tk):
    qi = pl.program_id(1)
    nh = q_ref.shape[0]
    m_ref[...] = jnp.full(m_ref.shape, NEG, F32)
    acc_ref[...] = jnp.zeros(acc_ref.shape, F32)
    kv_rows = lambda j: pl.ds(pl.multiple_of(j * tk, tk), tk)

    def probs(j, slot, masked):
        for hh in range(nh):
            s = lax.dot_general(q_ref[hh], k_ref[hh, kv_rows(j), :], (((1,), (1,)), ((), ())),
                                preferred_element_type=F32)
            if masked:
                row = lax.broadcasted_iota(jnp.int32, s.shape, 0)
                col = lax.broadcasted_iota(jnp.int32, s.shape, 1)
                s = jnp.where(col - row <= qi * tq - j * tk, s, NEG)
            m_old = m_ref[hh]
            m_new = jnp.maximum(m_old, jnp.max(s, axis=-1, keepdims=True))
            m_ref[hh] = m_new
            alpha_ref[slot, hh] = jnp.exp2(m_old - m_new)
            p_ref[slot, hh] = jnp.exp2(s - m_new).astype(BF16)

    def accumulate(j, slot):
        ones = jnp.ones((tk, HEAD_DIM), BF16)
        for hh in range(nh):
            v_ext = jnp.concatenate([v_ref[hh, kv_rows(j), :], ones], axis=-1)
            acc_ref[hh] = (alpha_ref[slot, hh] * acc_ref[hh]
                           + jnp.dot(p_ref[slot, hh], v_ext, preferred_element_type=F32))

    def step(j, slot, masked):
        probs(j + 1, 1 - slot, masked)
        accumulate(j, slot)

    n_last = (qi * tq) // tk
    n_plain = jnp.maximum(n_last - 1, 0)

    @pl.when(n_last >= 1)
    def _():
        probs(0, 0, False)

    @pl.when(n_last == 0)
    def _():
        probs(0, 0, True)

    def pair(i, carry):
        step(2 * i, 0, False)
        step(2 * i + 1, 1, False)
        return carry

    lax.fori_loop(0, n_plain // 2, pair, 0)

    @pl.when(n_plain % 2 == 1)
    def _():
        step(n_plain - 1, 0, False)

    for parity in (0, 1):
        @pl.when((n_last >= 1) & ((n_last - 1) % 2 == parity))
        def _():
            step(n_last - 1, parity, True)

    for parity in (0, 1):
        @pl.when(n_last % 2 == parity)
        def _():
            accumulate(n_last, parity)
    for hh in range(nh):
        acc = acc_ref[hh]
        o_ref[hh] = (acc[:, :HEAD_DIM] / acc[:, HEAD_DIM:]).astype(BF16)


def _attention(q, k, v):
    h, s, _ = q.shape
    tq = min(ATTN_TILE, s)
    tk = min(ATTN_KV_TILE, s)
    nh = ATTN_HEADS
    assert tk % tq == 0
    kv_bytes = nh * (_nbytes((s, QK_PAD), BF16) + _nbytes((s, HEAD_DIM), BF16))
    scratch = [pltpu.VMEM((nh, tq, 1), F32), pltpu.VMEM((nh, tq, QK_PAD), F32),
               pltpu.VMEM((2, nh, tq, tk), BF16), pltpu.VMEM((2, nh, tq, 1), F32)]
    scratch_bytes = nh * (3 * _nbytes((tq, V7X_LANES), F32) + _nbytes((tq, QK_PAD), F32)
                          + 2 * _nbytes((tq, tk), BF16))
    return pl.pallas_call(
        functools.partial(_attn_body, tq=tq, tk=tk),
        grid=(h // nh, s // tq),
        in_specs=[pl.BlockSpec((nh, tq, QK_PAD), lambda hh, i: (hh, i, 0)),
                  pl.BlockSpec((nh, s, QK_PAD), lambda hh, i: (hh, 0, 0), pipeline_mode=pl.Buffered(1)),
                  pl.BlockSpec((nh, s, HEAD_DIM), lambda hh, i: (hh, 0, 0), pipeline_mode=pl.Buffered(1))],
        out_specs=pl.BlockSpec((nh, tq, HEAD_DIM), lambda hh, i: (hh, i, 0)),
        out_shape=_sds((h, s, HEAD_DIM), BF16),
        scratch_shapes=scratch,
        compiler_params=pltpu.CompilerParams(
            dimension_semantics=("arbitrary", "arbitrary"),
            vmem_limit_bytes=_vmem_limit(kv_bytes, 4 * nh * _nbytes((tq, QK_PAD), BF16), scratch_bytes,
                                         3 * nh * _nbytes((tq, tk), F32))),
        name="attn",
    )(q, k, v)


def _hgrn_chunk(qf, kf, vb, b, st, c):
    nt = (((1,), (1,)), ((), ()))

    o = lax.dot_general((qf * jnp.exp2(b)).astype(BF16), st.astype(BF16), nt, preferred_element_type=F32)

    row = lax.broadcasted_iota(jnp.int32, (c, c), 0)
    col = lax.broadcasted_iota(jnp.int32, (c, c), 1)
    shp8 = (c // V7X_SUBLANES, V7X_SUBLANES, HEAD_DIM)
    pos8 = lax.broadcasted_iota(jnp.int32, shp8, 1)
    scores = None
    m = c // 2
    while m >= 1:
        if m >= V7X_SUBLANES:
            shp = (c // (2 * m), 2, m, HEAD_DIM)
            b4, q4, k4 = b.reshape(shp), qf.reshape(shp), kf.reshape(shp)
            mid = b4[:, 0, m - 1:m, :]
            q_second = jnp.exp2(b4[:, 1] - mid) * q4[:, 1]
            k_first = jnp.exp2(mid - b4[:, 0]) * k4[:, 0]
            zero = jnp.zeros_like(q_second)
            qt = jnp.stack([zero, q_second], axis=1).reshape(c, HEAD_DIM)
            kt = jnp.stack([k_first, zero], axis=1).reshape(c, HEAD_DIM)
        elif m >= 2:
            b3 = b.reshape(shp8)
            if m == 4:
                mid = b3[:, 3:4, :]
                second = pos8 >= 4
            else:
                mid = jnp.where(pos8 < 4, b3[:, 1:2, :], b3[:, 5:6, :])
                second = (pos8 & 3) >= 2
            d = b3 - mid
            qt = (jnp.exp2(jnp.where(second, d, NEG)) * qf.reshape(shp8)).reshape(c, HEAD_DIM)
            kt = (jnp.exp2(jnp.where(second, NEG, -d)) * kf.reshape(shp8)).reshape(c, HEAD_DIM)
        else:
            odd = (lax.broadcasted_iota(jnp.int32, b.shape, 0) & 1) == 1
            qt = jnp.exp2(jnp.where(odd, b - pltpu.roll(b, 1, 0), NEG)) * qf
            kt = jnp.where(odd, 0.0, kf)
        s_l = lax.dot_general(qt.astype(BF16), kt.astype(BF16), nt, preferred_element_type=F32)
        if scores is None:
            scores = s_l
        else:
            shift = (2 * m).bit_length() - 1
            scores = jnp.where((row >> shift) == (col >> shift), s_l, scores)
        m //= 2
    scores = jnp.where(row == col, jnp.sum(qf * kf, axis=-1, keepdims=True), scores)
    o = o + jnp.dot(scores.astype(BF16), vb, preferred_element_type=F32)

    b_last = b[c - 1:c, :]
    kd = (kf * jnp.exp2(b_last - b)).astype(BF16)
    st_new = st * jnp.exp2(b_last) + lax.dot_general(vb, kd, (((0,), (0,)), ((), ())),
                                                    preferred_element_type=F32)
    return o, st_new


def _hgrn_body(q_ref, b_ref, k_ref, v_ref, og_ref, gain_ref, o_ref, st_ref, *, c, rows):
    @pl.when(pl.program_id(1) == 0)
    def _():
        st_ref[...] = jnp.zeros_like(st_ref)

    gain = gain_ref[...]

    def step(i, carry):
        sl = pl.ds(pl.multiple_of(i * c, c), c)
        for hh in range(q_ref.shape[0]):
            o, st_new = _hgrn_chunk(q_ref[hh, sl, :].astype(F32), k_ref[hh, sl, :].astype(F32),
                                    v_ref[hh, sl, :], b_ref[hh, sl, :], st_ref[hh], c)
            st_ref[hh] = st_new
            o_ref[hh, sl, :] = (_rms(o, gain) * og_ref[hh, sl, :].astype(F32)).astype(BF16)
        return carry

    lax.fori_loop(0, rows // c, step, 0)


def _hgrn(q, b, k, v, og, gain):
    h, s, d = q.shape
    rows = min(HGRN_ROWS, s)
    c = HGRN_CHUNK
    hb = HGRN_HEADS
    spec = pl.BlockSpec((hb, rows, d), lambda hh, i: (hh, i, 0))
    return pl.pallas_call(
        functools.partial(_hgrn_body, c=c, rows=rows),
        grid=(h // hb, s // rows),
        in_specs=[spec, spec, spec, spec, spec, pl.BlockSpec((1, d), lambda hh, i: (0, 0))],
        out_specs=spec,
        out_shape=_sds((h, s, d), BF16),
        scratch_shapes=[pltpu.VMEM((hb, d, d), F32)],
        compiler_params=pltpu.CompilerParams(
            dimension_semantics=("arbitrary", "arbitrary"),
            vmem_limit_bytes=_vmem_limit(2 * 5 * hb * _nbytes((rows, d), F32), 64 * hb * _nbytes((c, d), F32))),
        name="hgrn",
    )(q, b, k, v, og, gain)


def _cat_heads(ref):
    return jnp.concatenate([ref[hh] for hh in range(ref.shape[0])], axis=-1)


def _mix_body(oa_ref, ob_ref, sa_ref, sb_ref, x_ref,
              woa_ref, wob_ref, wout_ref, gpost_ref, gate_ref, gpre_ref, sc_ref, sh_ref,
              x1_ref, h2_ref):
    ya = jnp.dot(_cat_heads(oa_ref), woa_ref[...], preferred_element_type=F32)
    yb = jnp.dot(_cat_heads(ob_ref), wob_ref[...], preferred_element_type=F32)
    merged = sa_ref[...].astype(F32) * ya + sb_ref[...].astype(F32) * yb
    y = jnp.dot(merged.astype(BF16), wout_ref[...], preferred_element_type=F32)
    x1 = x_ref[...] + gate_ref[...] * _rms(y, gpost_ref[...])
    x1_ref[...] = x1
    h2_ref[...] = (_rms(x1, gpre_ref[...]) * (1.0 + sc_ref[...]) + sh_ref[...]).astype(BF16)


def _gateup_body(h_ref, wg_ref, wu_ref, o_ref):
    h = h_ref[...]
    gte = jnp.dot(h, wg_ref[...].astype(BF16), preferred_element_type=F32)
    up = jnp.dot(h, wu_ref[...].astype(BF16), preferred_element_type=F32)
    o_ref[...] = (gte * jax.nn.sigmoid(gte) * up).astype(BF16)


def _gateup(h2, w_gu, d_ff):
    s, d = h2.shape
    tm = min(1024, s)
    tn = 512
    nj = d_ff // tn
    return pl.pallas_call(
        _gateup_body,
        grid=(s // tm, nj),
        in_specs=[pl.BlockSpec((tm, d), lambda i, j: (i, 0)),
                  pl.BlockSpec((None, d, tn), lambda i, j: (0, 0, j)),
                  pl.BlockSpec((None, d, tn), lambda i, j: (0, 0, j + nj))],
        out_specs=pl.BlockSpec((tm, tn), lambda i, j: (i, j)),
        out_shape=_sds((s, d_ff), BF16),
        compiler_params=pltpu.CompilerParams(
            dimension_semantics=("arbitrary", "arbitrary"),
            vmem_limit_bytes=_vmem_limit(2 * _nbytes((tm, d), BF16), 4 * _nbytes((d, tn), F32),
                                         2 * _nbytes((d, tn), BF16),
                                         2 * _nbytes((tm, tn), BF16), 4 * _nbytes((tm, tn), F32))),
        name="gateup",
    )(h2, w_gu, w_gu)


def _down_body(a_ref, x_ref, w_ref, gpost_ref, gate_ref, o_ref):
    y = jnp.dot(a_ref[...], w_ref[...], preferred_element_type=F32)
    o_ref[...] = x_ref[...] + gate_ref[...] * _rms(y, gpost_ref[...])


def kernel(x, c, positions, w_ada, b_ada, g_pre_mix, w_in, lb_logits, g_hg_out, w_o_hg, g_q_lora, w_uq,
           g_kv_lora, w_ukv, w_o_mla, w_out, g_post_mix, g_pre_ffn, w_gate_up, w_down, g_post_ffn):
    bsz, s, d = x.shape
    assert bsz == 1 and d == N_HEADS * HEAD_DIM
    d_ff = w_down.shape[1]
    x2 = x.reshape(s, d)
    row = lambda v: v.reshape(1, -1)
    hm = lambda dt: _sds((N_HEADS, s, HEAD_DIM), dt)

    w_in_t = jnp.swapaxes(w_in, 1, 2)
    o_cq, o_kr = 4 * d, 4 * d + 2 * LORA
    o_ga = o_kr + ROPE_DIM
    o_gb = o_ga + d
    w_uq_p = jnp.pad(w_uq[0].reshape(LORA, N_HEADS, QK_DIM),
                     ((0, 0), (0, 0), (0, QK_PAD - QK_DIM))).reshape(LORA, N_HEADS * QK_PAD).astype(BF16)
    w_ukv_b = w_ukv[0].astype(BF16)
    inv_freq = ROPE_THETA ** (-jnp.arange(0, ROPE_DIM, 2, dtype=F32) / ROPE_DIM)
    inv_row = jnp.tile(inv_freq, V7X_LANES // ROPE_HALF).reshape(1, V7X_LANES)

    mod = _ada(c.reshape(d, 1), w_ada[0], row(b_ada[0]))
    sh_m, sc_m, ga_m, sh_f, sc_f, ga_f = (mod[:, i * d:(i + 1) * d] for i in range(6))

    tm = min(1024, s)
    tab = _sds((s, V7X_LANES), F32)
    rc, ra, rb = _row_call(_ropetab_body, [positions.reshape(s, 1)], [inv_row], [tab, tab, tab],
                           tm, 8 * _nbytes((tm, V7X_LANES), F32), "ropetab")

    tm = min(512, s)
    (h,) = _row_call(_prenorm_body, [x2], [row(g_pre_mix[0]), sc_m, sh_m], [_sds((s, d), BF16)],
                     tm, 3 * _nbytes((tm, d), F32), "prenorm")

    tmp = 6 * _nbytes((tm, 512), F32)
    tm2 = min(1024, s)
    tmp2 = 4 * _nbytes((tm2, 512), F32)
    wsc = lambda n: [((n, d), BF16)]
    wt = lambda r0, n: (w_in_t, n, r0)
    (q_h,) = _row_call(_in_silu_body, [h], [wt(0, d)], [hm(BF16)], tm2, tmp2, "in_q", wsc(d))
    b_h, k_h = _row_call(_in_forget_body, [h], [wt(d, d), lb_logits], [hm(F32), hm(BF16)], tm, tmp, "in_f",
                         wsc(d))
    (v_h,) = _row_call(_in_plain_body, [h], [wt(2 * d, d)], [hm(BF16)], tm2, tmp2, "in_v", wsc(d))
    (og_h,) = _row_call(_in_silu_body, [h], [wt(3 * d, d)], [hm(BF16)], tm2, tmp2, "in_og", wsc(d))
    (cq_n,) = _row_call(_in_lora_body, [h], [wt(o_cq, LORA), row(g_q_lora[0])],
                        [_sds((s, LORA), BF16)], tm, tmp, "in_cq", wsc(LORA))
    (ckv_n,) = _row_call(_in_lora_body, [h], [wt(o_cq + LORA, LORA), row(g_kv_lora[0])],
                         [_sds((s, LORA), BF16)], tm, tmp, "in_ckv", wsc(LORA))
    (k_rope,) = _row_call(_in_krope_body, [h, rc, ra, rb], [wt(o_kr, V7X_LANES)],
                          [_sds((s, V7X_LANES), BF16)], tm, tmp, "in_kr", wsc(V7X_LANES))
    (sig_a,) = _row_call(_in_gate_body, [h], [wt(o_ga, d)], [_sds((s, d), BF16)], tm2, tmp2, "in_ga", wsc(d))
    (sig_b,) = _row_call(_in_gate_body, [h], [wt(o_gb, d)], [_sds((s, d), BF16)], tm2, tmp2, "in_gb", wsc(d))

    o_a = _hgrn(q_h, b_h, k_h, v_h, og_h, row(g_hg_out[0]))

    (q_m,) = _row_call(_qup_body, [cq_n, rc, ra, rb], [w_uq_p], [_sds((N_HEADS, s, QK_PAD), BF16)],
                       tm, tmp, "qup")
    k_m, v_m = _row_call(_kvup_body, [ckv_n, k_rope], [w_ukv_b],
                         [_sds((N_HEADS, s, QK_PAD), BF16), hm(BF16)], tm, tmp, "kvup")
    o_b = _attention(q_m, k_m, v_m)

    tm = min(256, s)
    x1, h2 = _row_call(
        _mix_body, [o_a, o_b, sig_a, sig_b, x2],
        [w_o_hg[0].astype(BF16), w_o_mla[0].astype(BF16), w_out[0].astype(BF16),
         row(g_post_mix[0]), ga_m, row(g_pre_ffn[0]), sc_f, sh_f],
        [_sds((s, d), F32), _sds((s, d), BF16)], tm, 8 * _nbytes((tm, d), F32), "mix")

    act = _gateup(h2, w_gate_up, d_ff)
    (out,) = _row_call(_down_body, [act, x1], [w_down[0].astype(BF16), row(g_post_ffn[0]), ga_f],
                       [_sds((s, d), F32)], tm, 4 * _nbytes((tm, d), F32), "down")
    return out.reshape(bsz, s, d)
```

```python
import functools

import jax
import jax.numpy as jnp
from jax import lax
from jax.experimental import pallas as pl
from jax.experimental.pallas import tpu as pltpu

F32 = jnp.float32
BF16 = jnp.bfloat16

V7X_VMEM_BYTES = 64 * 1024 * 1024
V7X_LANES = 128
V7X_SUBLANES = 8

EPS = 1e-6
N_HEADS = 16
HEAD_DIM = 128
ROPE_DIM = 64
ROPE_HALF = ROPE_DIM // 2
QK_DIM = HEAD_DIM + ROPE_DIM
QK_PAD = 2 * HEAD_DIM
LORA = 512
ROPE_THETA = 10000.0
NEG = -1e30
LOG2E = 1.4426950408889634

HGRN_CHUNK = 128
HGRN_ROWS = 1024
HGRN_HEADS = 8
ATTN_TILE = 512
ATTN_KV_TILE = 1024
ATTN_HEADS = 4


def _vmem_limit(*nbytes):
    need = int(sum(nbytes))
    return int(min(V7X_VMEM_BYTES - (4 << 20), need + (8 << 20)))


def _nbytes(shape, dtype):
    n = 1
    for s in shape:
        n *= s
    return n * jnp.dtype(dtype).itemsize


def _row_spec(arr_shape, tm):
    if len(arr_shape) == 2:
        return pl.BlockSpec((tm, arr_shape[1]), lambda i: (i, 0))
    return pl.BlockSpec((arr_shape[0], tm, arr_shape[2]), lambda i: (0, i, 0))


def _const_spec(arr_shape):
    nd = len(arr_shape)
    return pl.BlockSpec(tuple(arr_shape), lambda i: (0,) * nd)


def _row_call(body, row_ins, const_ins, outs, tm, temp_bytes, name, scratch=()):
    m = row_ins[0].shape[-2]
    assert m % tm == 0, (m, tm)
    in_specs = [_row_spec(a.shape, tm) for a in row_ins]
    const_bytes = 0
    const_args = []
    for cin in const_ins:
        if isinstance(cin, tuple):
            arr, height, r0 = cin
            in_specs.append(pl.BlockSpec((None, pl.Element(height), pl.Element(arr.shape[2])),
                                         lambda i, r0=r0: (0, r0, 0), pipeline_mode=pl.Buffered(1)))
            const_bytes += _nbytes((height, arr.shape[2]), arr.dtype)
        else:
            arr = cin
            in_specs.append(_const_spec(arr.shape))
            const_bytes += _nbytes(arr.shape, arr.dtype)
        const_args.append(arr)
    out_specs = [_row_spec(o.shape, tm) for o in outs]
    tile = lambda s: tuple(s[:-2]) + (tm, s[-1])
    need = (2 * sum(_nbytes(tile(a.shape), a.dtype) for a in row_ins)
            + 2 * sum(_nbytes(tile(o.shape), o.dtype) for o in outs)
            + const_bytes + sum(_nbytes(shp, dt) for shp, dt in scratch))
    res = pl.pallas_call(
        body,
        grid=(m // tm,),
        in_specs=in_specs,
        out_specs=out_specs,
        out_shape=outs,
        scratch_shapes=[pltpu.VMEM(shp, dt) for shp, dt in scratch],
        compiler_params=pltpu.CompilerParams(
            dimension_semantics=("arbitrary",),
            vmem_limit_bytes=_vmem_limit(need, temp_bytes)),
        name=name,
    )(*row_ins, *const_args)
    return res


def _weights_bf16(w_ref, wb_ref):
    @pl.when(pl.program_id(0) == 0)
    def _():
        step = min(256, w_ref.shape[0])
        for r0 in range(0, w_ref.shape[0], step):
            wb_ref[r0:r0 + step, :] = w_ref[r0:r0 + step, :].astype(BF16)


def _mm_chunks_t(a, wt_ref, nc):
    n = wt_ref.shape[0]
    for c0 in range(0, n, nc):
        yield c0, lax.dot_general(a, wt_ref[c0:min(c0 + nc, n), :], (((1,), (1,)), ((), ())),
                                  preferred_element_type=F32)


def _sds(shape, dtype):
    return jax.ShapeDtypeStruct(tuple(shape), dtype)


def _rms(y, gain):
    return y * lax.rsqrt(jnp.mean(y * y, axis=-1, keepdims=True) + EPS) * gain


def _rope128(x, c, a, b):
    return x * c + pltpu.roll(x, 96, 1) * a + pltpu.roll(x, 32, 1) * b


def _ada_body(c_ref, w_ref, b_ref, o_ref):
    cc = c_ref[...]
    sc = cc * jax.nn.sigmoid(cc)
    o_ref[...] = jnp.sum(sc * w_ref[...], axis=0, keepdims=True) + b_ref[...]


def _ada(c_col, w, b_row):
    d, n = w.shape
    tn = 1024
    return pl.pallas_call(
        _ada_body,
        grid=(n // tn,),
        in_specs=[pl.BlockSpec((d, 1), lambda j: (0, 0)),
                  pl.BlockSpec((d, tn), lambda j: (0, j)),
                  pl.BlockSpec((1, tn), lambda j: (0, j))],
        out_specs=pl.BlockSpec((1, tn), lambda j: (0, j)),
        out_shape=_sds((1, n), F32),
        compiler_params=pltpu.CompilerParams(
            dimension_semantics=("arbitrary",),
            vmem_limit_bytes=_vmem_limit(3 * _nbytes((d, tn), F32), _nbytes((d, V7X_LANES), F32))),
        name="ada",
    )(c_col, w, b_row)


def _ropetab_body(pos_ref, inv_ref, c_ref, a_ref, b_ref):
    ang = pos_ref[...].astype(F32) * inv_ref[...]
    cos = jnp.cos(ang)
    sin = jnp.sin(ang)
    lane = lax.broadcasted_iota(jnp.int32, ang.shape, 1)
    c_ref[...] = jnp.where(lane < ROPE_DIM, cos, 0.0)
    a_ref[...] = jnp.where(lane < ROPE_HALF, -sin, 0.0)
    b_ref[...] = jnp.where((lane >= ROPE_HALF) & (lane < ROPE_DIM), sin, 0.0)


def _prenorm_body(x_ref, g_ref, sc_ref, sh_ref, h_ref):
    x = x_ref[...]
    h_ref[...] = (_rms(x, g_ref[...]) * (1.0 + sc_ref[...]) + sh_ref[...]).astype(BF16)


def _heads_out(o_ref, c0, val):
    for hh in range(val.shape[1] // HEAD_DIM):
        o_ref[c0 // HEAD_DIM + hh] = val[:, hh * HEAD_DIM:(hh + 1) * HEAD_DIM]


def _mm_chunks(a, w_ref, nc):
    n = w_ref.shape[1]
    for c0 in range(0, n, nc):
        yield c0, jnp.dot(a, w_ref[:, c0:c0 + nc], preferred_element_type=F32)


def _in_silu_body(h_ref, w_ref, o_ref, wb_ref):
    _weights_bf16(w_ref, wb_ref)
    for c0, r in _mm_chunks_t(h_ref[...], wb_ref, 512):
        _heads_out(o_ref, c0, (r * jax.nn.sigmoid(r)).astype(BF16))


def _in_plain_body(h_ref, w_ref, o_ref, wb_ref):
    _weights_bf16(w_ref, wb_ref)
    for c0, r in _mm_chunks_t(h_ref[...], wb_ref, 512):
        _heads_out(o_ref, c0, r.astype(BF16))


def _chunk_cumsum(g, c):
    rows, width = g.shape
    sub = V7X_SUBLANES
    x = g.reshape(rows // sub, sub, width)
    pos = lax.broadcasted_iota(jnp.int32, x.shape, 1)
    sh = 1
    while sh < sub:
        x = x + jnp.where(pos >= sh, pltpu.roll(x, sh, 1), 0.0)
        sh *= 2
    x = x.reshape(rows // c, c // sub, sub, width)
    run = None
    out = []
    for k in range(c // sub):
        blk = x[:, k]
        blk = blk if run is None else blk + run
        out.append(blk)
        run = blk[:, sub - 1:sub, :]
    return jnp.stack(out, axis=1).reshape(rows, width)


def _in_forget_body(h_ref, w_ref, lb_ref, b_ref, k_ref, wb_ref):
    _weights_bf16(w_ref, wb_ref)
    logits = lb_ref[...]
    e = jnp.exp(logits - jnp.max(logits, axis=0, keepdims=True))
    lb_all = e[0:1, :] / jnp.sum(e, axis=0, keepdims=True)
    for c0, r in _mm_chunks_t(h_ref[...], wb_ref, 512):
        lb = lb_all[:, c0:c0 + 512]
        f = lb + (1.0 - lb) * jax.nn.sigmoid(r)
        _heads_out(b_ref, c0, _chunk_cumsum(jnp.log2(f), HGRN_CHUNK))
        _heads_out(k_ref, c0, (1.0 - f).astype(BF16))


def _mla_in_body(h_ref, c_ref, a_ref, b_ref, w_ref, gq_ref, gkv_ref, wuq_ref, wukv_ref,
                 q_ref, k_ref, v_ref, wb_ref):
    _weights_bf16(w_ref, wb_ref)
    h = h_ref[...]
    c, a, b = c_ref[...], a_ref[...], b_ref[...]
    lat = dict(_mm_chunks_t(h, wb_ref, LORA))
    cq = _rms(lat[0], gq_ref[...]).astype(BF16)
    ckv = _rms(lat[LORA], gkv_ref[...]).astype(BF16)
    kr = _rope128(lat[2 * LORA], c, a, b).astype(BF16)

    scale = QK_DIM ** -0.5 * LOG2E
    for c0, r in _mm_chunks(cq, wuq_ref, 2 * QK_PAD):
        for hh in range(2):
            head = c0 // QK_PAD + hh
            nope = r[:, hh * QK_PAD:hh * QK_PAD + HEAD_DIM]
            rope = _rope128(r[:, hh * QK_PAD + HEAD_DIM:(hh + 1) * QK_PAD], c, a, b)
            q_ref[head, :, 0:HEAD_DIM] = (nope * scale).astype(BF16)
            q_ref[head, :, HEAD_DIM:QK_PAD] = (rope * scale).astype(BF16)
    for c0, r in _mm_chunks(ckv, wukv_ref, 2 * QK_PAD):
        for hh in range(2):
            head = c0 // QK_PAD + hh
            k_ref[head, :, 0:HEAD_DIM] = r[:, hh * QK_PAD:hh * QK_PAD + HEAD_DIM].astype(BF16)
            k_ref[head, :, HEAD_DIM:QK_PAD] = kr
            v_ref[head] = r[:, hh * QK_PAD + HEAD_DIM:(hh + 1) * QK_PAD].astype(BF16)


def _in_gate_body(h_ref, w_ref, o_ref, wb_ref):
    _weights_bf16(w_ref, wb_ref)
    for c0, r in _mm_chunks_t(h_ref[...], wb_ref, 512):
        o_ref[:, c0:c0 + 512] = jax.nn.sigmoid(r).astype(BF16)


def _attn_body(q_ref, k_ref, v_ref, o_ref, m_ref, acc_ref, p_ref, alpha_ref, *, tq, tk):
    qi = pl.program_id(1)
    nh = q_ref.shape[0]
    m_ref[...] = jnp.full(m_ref.shape, NEG, F32)
    acc_ref[...] = jnp.zeros(acc_ref.shape, F32)
    kv_rows = lambda j: pl.ds(pl.multiple_of(j * tk, tk), tk)

    def probs(j, slot, masked):
        for hh in range(nh):
            s = lax.dot_general(q_ref[hh], k_ref[hh, kv_rows(j), :], (((1,), (1,)), ((), ())),
                                preferred_element_type=F32)
            if masked:
                row = lax.broadcasted_iota(jnp.int32, s.shape, 0)
                col = lax.broadcasted_iota(jnp.int32, s.shape, 1)
                s = jnp.where(col - row <= qi * tq - j * tk, s, NEG)
            m_old = m_ref[hh]
            m_new = jnp.maximum(m_old, jnp.max(s, axis=-1, keepdims=True))
            m_ref[hh] = m_new
            alpha_ref[slot, hh] = jnp.exp2(m_old - m_new)
            p_ref[slot, hh] = jnp.exp2(s - m_new).astype(BF16)

    def accumulate(j, slot):
        ones = jnp.ones((tk, HEAD_DIM), BF16)
        for hh in range(nh):
            v_ext = jnp.concatenate([v_ref[hh, kv_rows(j), :], ones], axis=-1)
            acc_ref[hh] = (alpha_ref[slot, hh] * acc_ref[hh]
                           + jnp.dot(p_ref[slot, hh], v_ext, preferred_element_type=F32))

    def step(j, slot, masked):
        probs(j + 1, 1 - slot, masked)
        accumulate(j, slot)

    n_last = (qi * tq) // tk
    n_plain = jnp.maximum(n_last - 1, 0)

    @pl.when(n_last >= 1)
    def _():
        probs(0, 0, False)

    @pl.when(n_last == 0)
    def _():
        probs(0, 0, True)

    def pair(i, carry):
        step(2 * i, 0, False)
        step(2 * i + 1, 1, False)
        return carry

    lax.fori_loop(0, n_plain // 2, pair, 0)

    @pl.when(n_plain % 2 == 1)
    def _():
        step(n_plain - 1, 0, False)

    for parity in (0, 1):
        @pl.when((n_last >= 1) & ((n_last - 1) % 2 == parity))
        def _():
            step(n_last - 1, parity, True)

    for parity in (0, 1):
        @pl.when(n_last % 2 == parity)
        def _():
            accumulate(n_last, parity)
    for hh in range(nh):
        acc = acc_ref[hh]
        o_ref[hh] = (acc[:, :HEAD_DIM] / acc[:, HEAD_DIM:]).astype(BF16)


def _attention(q, k, v):
    h, s, _ = q.shape
    tq = min(ATTN_TILE, s)
    tk = min(ATTN_KV_TILE, s)
    nh = ATTN_HEADS
    assert tk % tq == 0
    kv_bytes = nh * (_nbytes((s, QK_PAD), BF16) + _nbytes((s, HEAD_DIM), BF16))
    scratch = [pltpu.VMEM((nh, tq, 1), F32), pltpu.VMEM((nh, tq, QK_PAD), F32),
               pltpu.VMEM((2, nh, tq, tk), BF16), pltpu.VMEM((2, nh, tq, 1), F32)]
    scratch_bytes = nh * (3 * _nbytes((tq, V7X_LANES), F32) + _nbytes((tq, QK_PAD), F32)
                          + 2 * _nbytes((tq, tk), BF16))
    return pl.pallas_call(
        functools.partial(_attn_body, tq=tq, tk=tk),
        grid=(h // nh, s // tq),
        in_specs=[pl.BlockSpec((nh, tq, QK_PAD), lambda hh, i: (hh, i, 0)),
                  pl.BlockSpec((nh, s, QK_PAD), lambda hh, i: (hh, 0, 0), pipeline_mode=pl.Buffered(1)),
                  pl.BlockSpec((nh, s, HEAD_DIM), lambda hh, i: (hh, 0, 0), pipeline_mode=pl.Buffered(1))],
        out_specs=pl.BlockSpec((nh, tq, HEAD_DIM), lambda hh, i: (hh, i, 0)),
        out_shape=_sds((h, s, HEAD_DIM), BF16),
        scratch_shapes=scratch,
        compiler_params=pltpu.CompilerParams(
            dimension_semantics=("arbitrary", "arbitrary"),
            vmem_limit_bytes=_vmem_limit(kv_bytes, 4 * nh * _nbytes((tq, QK_PAD), BF16), scratch_bytes,
                                         3 * nh * _nbytes((tq, tk), F32))),
        name="attn",
    )(q, k, v)


def _hgrn_chunk(qf, kf, vb, b, st, c):
    nt = (((1,), (1,)), ((), ()))

    o = lax.dot_general((qf * jnp.exp2(b)).astype(BF16), st.astype(BF16), nt, preferred_element_type=F32)

    row = lax.broadcasted_iota(jnp.int32, (c, c), 0)
    col = lax.broadcasted_iota(jnp.int32, (c, c), 1)
    shp8 = (c // V7X_SUBLANES, V7X_SUBLANES, HEAD_DIM)
    pos8 = lax.broadcasted_iota(jnp.int32, shp8, 1)
    scores = None
    m = c // 2
    while m >= 1:
        if m >= V7X_SUBLANES:
            shp = (c // (2 * m), 2, m, HEAD_DIM)
            b4, q4, k4 = b.reshape(shp), qf.reshape(shp), kf.reshape(shp)
            mid = b4[:, 0, m - 1:m, :]
            q_second = jnp.exp2(b4[:, 1] - mid) * q4[:, 1]
            k_first = jnp.exp2(mid - b4[:, 0]) * k4[:, 0]
            zero = jnp.zeros_like(q_second)
            qt = jnp.stack([zero, q_second], axis=1).reshape(c, HEAD_DIM)
            kt = jnp.stack([k_first, zero], axis=1).reshape(c, HEAD_DIM)
        elif m >= 2:
            b3 = b.reshape(shp8)
            if m == 4:
                mid = b3[:, 3:4, :]
                second = pos8 >= 4
            else:
                mid = jnp.where(pos8 < 4, b3[:, 1:2, :], b3[:, 5:6, :])
                second = (pos8 & 3) >= 2
            d = b3 - mid
            qt = (jnp.exp2(jnp.where(second, d, NEG)) * qf.reshape(shp8)).reshape(c, HEAD_DIM)
            kt = (jnp.exp2(jnp.where(second, NEG, -d)) * kf.reshape(shp8)).reshape(c, HEAD_DIM)
        else:
            odd = (lax.broadcasted_iota(jnp.int32, b.shape, 0) & 1) == 1
            qt = jnp.exp2(jnp.where(odd, b - pltpu.roll(b, 1, 0), NEG)) * qf
            kt = jnp.where(odd, 0.0, kf)
        s_l = lax.dot_general(qt.astype(BF16), kt.astype(BF16), nt, preferred_element_type=F32)
        if scores is None:
            scores = s_l
        else:
            shift = (2 * m).bit_length() - 1
            scores = jnp.where((row >> shift) == (col >> shift), s_l, scores)
        m //= 2
    scores = jnp.where(row == col, jnp.sum(qf * kf, axis=-1, keepdims=True), scores)
    o = o + jnp.dot(scores.astype(BF16), vb, preferred_element_type=F32)

    b_last = b[c - 1:c, :]
    kd = (kf * jnp.exp2(b_last - b)).astype(BF16)
    st_new = st * jnp.exp2(b_last) + lax.dot_general(vb, kd, (((0,), (0,)), ((), ())),
                                                    preferred_element_type=F32)
    return o, st_new


def _hgrn_body(q_ref, b_ref, k_ref, v_ref, og_ref, gain_ref, o_ref, st_ref, *, c, rows):
    @pl.when(pl.program_id(1) == 0)
    def _():
        st_ref[...] = jnp.zeros_like(st_ref)

    gain = gain_ref[...]

    def step(i, carry):
        sl = pl.ds(pl.multiple_of(i * c, c), c)
        for hh in range(q_ref.shape[0]):
            o, st_new = _hgrn_chunk(q_ref[hh, sl, :].astype(F32), k_ref[hh, sl, :].astype(F32),
                                    v_ref[hh, sl, :], b_ref[hh, sl, :], st_ref[hh], c)
            st_ref[hh] = st_new
            o_ref[hh, sl, :] = (_rms(o, gain) * og_ref[hh, sl, :].astype(F32)).astype(BF16)
        return carry

    lax.fori_loop(0, rows // c, step, 0)


def _hgrn(q, b, k, v, og, gain):
    h, s, d = q.shape
    rows = min(HGRN_ROWS, s)
    c = HGRN_CHUNK
    hb = HGRN_HEADS
    spec = pl.BlockSpec((hb, rows, d), lambda hh, i: (hh, i, 0))
    return pl.pallas_call(
        functools.partial(_hgrn_body, c=c, rows=rows),
        grid=(h // hb, s // rows),
        in_specs=[spec, spec, spec, spec, spec, pl.BlockSpec((1, d), lambda hh, i: (0, 0))],
        out_specs=spec,
        out_shape=_sds((h, s, d), BF16),
        scratch_shapes=[pltpu.VMEM((hb, d, d), F32)],
        compiler_params=pltpu.CompilerParams(
            dimension_semantics=("arbitrary", "arbitrary"),
            vmem_limit_bytes=_vmem_limit(2 * 5 * hb * _nbytes((rows, d), F32), 64 * hb * _nbytes((c, d), F32))),
        name="hgrn",
    )(q, b, k, v, og, gain)


def _cat_heads(ref):
    return jnp.concatenate([ref[hh] for hh in range(ref.shape[0])], axis=-1)


def _mix_body(oa_ref, ob_ref, sa_ref, sb_ref, x_ref,
              woa_ref, wob_ref, wout_ref, gpost_ref, gate_ref, gpre_ref, sc_ref, sh_ref,
              x1_ref, h2_ref):
    ya = jnp.dot(_cat_heads(oa_ref), woa_ref[...], preferred_element_type=F32)
    yb = jnp.dot(_cat_heads(ob_ref), wob_ref[...], preferred_element_type=F32)
    merged = sa_ref[...].astype(F32) * ya + sb_ref[...].astype(F32) * yb
    y = jnp.dot(merged.astype(BF16), wout_ref[...], preferred_element_type=F32)
    x1 = x_ref[...] + gate_ref[...] * _rms(y, gpost_ref[...])
    x1_ref[...] = x1
    h2_ref[...] = (_rms(x1, gpre_ref[...]) * (1.0 + sc_ref[...]) + sh_ref[...]).astype(BF16)


def _gateup_body(h_ref, wg_ref, wu_ref, o_ref):
    h = h_ref[...]
    gte = jnp.dot(h, wg_ref[...].astype(BF16), preferred_element_type=F32)
    up = jnp.dot(h, wu_ref[...].astype(BF16), preferred_element_type=F32)
    o_ref[...] = (gte * jax.nn.sigmoid(gte) * up).astype(BF16)


def _gateup(h2, w_gu, d_ff):
    s, d = h2.shape
    tm = min(1024, s)
    tn = 512
    nj = d_ff // tn
    return pl.pallas_call(
        _gateup_body,
        grid=(s // tm, nj),
        in_specs=[pl.BlockSpec((tm, d), lambda i, j: (i, 0)),
                  pl.BlockSpec((None, d, tn), lambda i, j: (0, 0, j)),
                  pl.BlockSpec((None, d, tn), lambda i, j: (0, 0, j + nj))],
        out_specs=pl.BlockSpec((tm, tn), lambda i, j: (i, j)),
        out_shape=_sds((s, d_ff), BF16),
        compiler_params=pltpu.CompilerParams(
            dimension_semantics=("arbitrary", "arbitrary"),
            vmem_limit_bytes=_vmem_limit(2 * _nbytes((tm, d), BF16), 4 * _nbytes((d, tn), F32),
                                         2 * _nbytes((d, tn), BF16),
                                         2 * _nbytes((tm, tn), BF16), 4 * _nbytes((tm, tn), F32))),
        name="gateup",
    )(h2, w_gu, w_gu)


def _down_body(a_ref, x_ref, w_ref, gpost_ref, gate_ref, o_ref):
    y = jnp.dot(a_ref[...], w_ref[...], preferred_element_type=F32)
    o_ref[...] = x_ref[...] + gate_ref[...] * _rms(y, gpost_ref[...])


def kernel(x, c, positions, w_ada, b_ada, g_pre_mix, w_in, lb_logits, g_hg_out, w_o_hg, g_q_lora, w_uq,
           g_kv_lora, w_ukv, w_o_mla, w_out, g_post_mix, g_pre_ffn, w_gate_up, w_down, g_post_ffn):
    bsz, s, d = x.shape
    assert bsz == 1 and d == N_HEADS * HEAD_DIM
    d_ff = w_down.shape[1]
    x2 = x.reshape(s, d)
    row = lambda v: v.reshape(1, -1)
    hm = lambda dt: _sds((N_HEADS, s, HEAD_DIM), dt)

    w_in_t = jnp.swapaxes(w_in, 1, 2)
    o_cq, o_kr = 4 * d, 4 * d + 2 * LORA
    o_ga = o_kr + ROPE_DIM
    o_gb = o_ga + d
    w_uq_p = jnp.pad(w_uq[0].reshape(LORA, N_HEADS, QK_DIM),
                     ((0, 0), (0, 0), (0, QK_PAD - QK_DIM))).reshape(LORA, N_HEADS * QK_PAD).astype(BF16)
    w_ukv_b = w_ukv[0].astype(BF16)
    inv_freq = ROPE_THETA ** (-jnp.arange(0, ROPE_DIM, 2, dtype=F32) / ROPE_DIM)
    inv_row = jnp.tile(inv_freq, V7X_LANES // ROPE_HALF).reshape(1, V7X_LANES)

    mod = _ada(c.reshape(d, 1), w_ada[0], row(b_ada[0]))
    sh_m, sc_m, ga_m, sh_f, sc_f, ga_f = (mod[:, i * d:(i + 1) * d] for i in range(6))

    tm = min(1024, s)
    tab = _sds((s, V7X_LANES), F32)
    rc, ra, rb = _row_call(_ropetab_body, [positions.reshape(s, 1)], [inv_row], [tab, tab, tab],
                           tm, 8 * _nbytes((tm, V7X_LANES), F32), "ropetab")

    tm = min(512, s)
    (h,) = _row_call(_prenorm_body, [x2], [row(g_pre_mix[0]), sc_m, sh_m], [_sds((s, d), BF16)],
                     tm, 3 * _nbytes((tm, d), F32), "prenorm")

    tmp = 6 * _nbytes((tm, 512), F32)
    tm2 = min(1024, s)
    tmp2 = 4 * _nbytes((tm2, 512), F32)
    wsc = lambda n: [((n, d), BF16)]
    wt = lambda r0, n: (w_in_t, n, r0)
    (q_h,) = _row_call(_in_silu_body, [h], [wt(0, d)], [hm(BF16)], tm2, tmp2, "in_q", wsc(d))
    b_h, k_h = _row_call(_in_forget_body, [h], [wt(d, d), lb_logits], [hm(F32), hm(BF16)], tm, tmp, "in_f",
                         wsc(d))
    (v_h,) = _row_call(_in_plain_body, [h], [wt(2 * d, d)], [hm(BF16)], tm2, tmp2, "in_v", wsc(d))
    (og_h,) = _row_call(_in_silu_body, [h], [wt(3 * d, d)], [hm(BF16)], tm2, tmp2, "in_og", wsc(d))
    n_lat = 2 * LORA + V7X_LANES
    q_m, k_m, v_m = _row_call(
        _mla_in_body, [h, rc, ra, rb],
        [wt(o_cq, n_lat), row(g_q_lora[0]), row(g_kv_lora[0]), w_uq_p, w_ukv_b],
        [_sds((N_HEADS, s, QK_PAD), BF16), _sds((N_HEADS, s, QK_PAD), BF16), hm(BF16)],
        tm, tmp, "mla_in", wsc(n_lat))
    (sig_a,) = _row_call(_in_gate_body, [h], [wt(o_ga, d)], [_sds((s, d), BF16)], tm2, tmp2, "in_ga", wsc(d))
    (sig_b,) = _row_call(_in_gate_body, [h], [wt(o_gb, d)], [_sds((s, d), BF16)], tm2, tmp2, "in_gb", wsc(d))

    o_a = _hgrn(q_h, b_h, k_h, v_h, og_h, row(g_hg_out[0]))

    o_b = _attention(q_m, k_m, v_m)

    tm = min(256, s)
    x1, h2 = _row_call(
        _mix_body, [o_a, o_b, sig_a, sig_b, x2],
        [w_o_hg[0].astype(BF16), w_o_mla[0].astype(BF16), w_out[0].astype(BF16),
         row(g_post_mix[0]), ga_m, row(g_pre_ffn[0]), sc_f, sh_f],
        [_sds((s, d), F32), _sds((s, d), BF16)], tm, 8 * _nbytes((tm, d), F32), "mix")

    act = _gateup(h2, w_gate_up, d_ff)
    (out,) = _row_call(_down_body, [act, x1], [w_down[0].astype(BF16), row(g_post_ffn[0]), ga_f],
                       [_sds((s, d), F32)], tm, 4 * _nbytes((tm, d), F32), "down")
    return out.reshape(bsz, s, d)
```

```python
import functools

import jax
import jax.numpy as jnp
from jax import lax
from jax.experimental import pallas as pl
from jax.experimental.pallas import tpu as pltpu

F32 = jnp.float32
BF16 = jnp.bfloat16

V7X_VMEM_BYTES = 64 * 1024 * 1024
V7X_LANES = 128
V7X_SUBLANES = 8

EPS = 1e-6
N_HEADS = 16
HEAD_DIM = 128
ROPE_DIM = 64
ROPE_HALF = ROPE_DIM // 2
QK_DIM = HEAD_DIM + ROPE_DIM
QK_PAD = 2 * HEAD_DIM
LORA = 512
ROPE_THETA = 10000.0
NEG = -1e30
LOG2E = 1.4426950408889634

HGRN_CHUNK = 128
HGRN_ROWS = 1024
HGRN_HEADS = 8
ATTN_TILE = 512
ATTN_KV_TILE = 1024
ATTN_HEADS = 4


def _vmem_limit(*nbytes):
    need = int(sum(nbytes))
    return int(min(V7X_VMEM_BYTES - (4 << 20), need + (8 << 20)))


def _nbytes(shape, dtype):
    n = 1
    for s in shape:
        n *= s
    return n * jnp.dtype(dtype).itemsize


def _row_spec(arr_shape, tm):
    if len(arr_shape) == 2:
        return pl.BlockSpec((tm, arr_shape[1]), lambda i: (i, 0))
    return pl.BlockSpec((arr_shape[0], tm, arr_shape[2]), lambda i: (0, i, 0))


def _const_spec(arr_shape):
    nd = len(arr_shape)
    return pl.BlockSpec(tuple(arr_shape), lambda i: (0,) * nd)


def _row_call(body, row_ins, const_ins, outs, tm, temp_bytes, name, scratch=()):
    m = row_ins[0].shape[-2]
    assert m % tm == 0, (m, tm)
    in_specs = [_row_spec(a.shape, tm) for a in row_ins]
    const_bytes = 0
    const_args = []
    for cin in const_ins:
        if isinstance(cin, tuple):
            arr, height, r0 = cin
            in_specs.append(pl.BlockSpec((None, pl.Element(height), pl.Element(arr.shape[2])),
                                         lambda i, r0=r0: (0, r0, 0), pipeline_mode=pl.Buffered(1)))
            const_bytes += _nbytes((height, arr.shape[2]), arr.dtype)
        else:
            arr = cin
            in_specs.append(_const_spec(arr.shape))
            const_bytes += _nbytes(arr.shape, arr.dtype)
        const_args.append(arr)
    out_specs = [_row_spec(o.shape, tm) for o in outs]
    tile = lambda s: tuple(s[:-2]) + (tm, s[-1])
    need = (2 * sum(_nbytes(tile(a.shape), a.dtype) for a in row_ins)
            + 2 * sum(_nbytes(tile(o.shape), o.dtype) for o in outs)
            + const_bytes + sum(_nbytes(shp, dt) for shp, dt in scratch))
    res = pl.pallas_call(
        body,
        grid=(m // tm,),
        in_specs=in_specs,
        out_specs=out_specs,
        out_shape=outs,
        scratch_shapes=[pltpu.VMEM(shp, dt) for shp, dt in scratch],
        compiler_params=pltpu.CompilerParams(
            dimension_semantics=("arbitrary",),
            vmem_limit_bytes=_vmem_limit(need, temp_bytes)),
        name=name,
    )(*row_ins, *const_args)
    return res


def _weights_bf16(w_ref, wb_ref):
    @pl.when(pl.program_id(0) == 0)
    def _():
        step = min(256, w_ref.shape[0])
        for r0 in range(0, w_ref.shape[0], step):
            wb_ref[r0:r0 + step, :] = w_ref[r0:r0 + step, :].astype(BF16)


def _mm_chunks_t(a, wt_ref, nc):
    n = wt_ref.shape[0]
    for c0 in range(0, n, nc):
        yield c0, lax.dot_general(a, wt_ref[c0:min(c0 + nc, n), :], (((1,), (1,)), ((), ())),
                                  preferred_element_type=F32)


def _sds(shape, dtype):
    return jax.ShapeDtypeStruct(tuple(shape), dtype)


def _rms(y, gain):
    return y * lax.rsqrt(jnp.mean(y * y, axis=-1, keepdims=True) + EPS) * gain


def _rope128(x, c, a, b):
    return x * c + pltpu.roll(x, 96, 1) * a + pltpu.roll(x, 32, 1) * b


def _ada_body(c_ref, w_ref, b_ref, o_ref):
    cc = c_ref[...]
    sc = cc * jax.nn.sigmoid(cc)
    o_ref[...] = jnp.sum(sc * w_ref[...], axis=0, keepdims=True) + b_ref[...]


def _ada(c_col, w, b_row):
    d, n = w.shape
    tn = 1024
    return pl.pallas_call(
        _ada_body,
        grid=(n // tn,),
        in_specs=[pl.BlockSpec((d, 1), lambda j: (0, 0)),
                  pl.BlockSpec((d, tn), lambda j: (0, j)),
                  pl.BlockSpec((1, tn), lambda j: (0, j))],
        out_specs=pl.BlockSpec((1, tn), lambda j: (0, j)),
        out_shape=_sds((1, n), F32),
        compiler_params=pltpu.CompilerParams(
            dimension_semantics=("arbitrary",),
            vmem_limit_bytes=_vmem_limit(3 * _nbytes((d, tn), F32), _nbytes((d, V7X_LANES), F32))),
        name="ada",
    )(c_col, w, b_row)


def _ropetab_body(pos_ref, inv_ref, c_ref, a_ref, b_ref):
    ang = pos_ref[...].astype(F32) * inv_ref[...]
    cos = jnp.cos(ang)
    sin = jnp.sin(ang)
    lane = lax.broadcasted_iota(jnp.int32, ang.shape, 1)
    c_ref[...] = jnp.where(lane < ROPE_DIM, cos, 0.0)
    a_ref[...] = jnp.where(lane < ROPE_HALF, -sin, 0.0)
    b_ref[...] = jnp.where((lane >= ROPE_HALF) & (lane < ROPE_DIM), sin, 0.0)


def _heads_out(o_ref, c0, val):
    for hh in range(val.shape[1] // HEAD_DIM):
        o_ref[c0 // HEAD_DIM + hh] = val[:, hh * HEAD_DIM:(hh + 1) * HEAD_DIM]


def _mm_chunks(a, w_ref, nc):
    n = w_ref.shape[1]
    for c0 in range(0, n, nc):
        yield c0, jnp.dot(a, w_ref[:, c0:c0 + nc], preferred_element_type=F32)


def _in_silu_body(h_ref, w_ref, o_ref, wb_ref):
    _weights_bf16(w_ref, wb_ref)
    for c0, r in _mm_chunks_t(h_ref[...], wb_ref, 512):
        _heads_out(o_ref, c0, (r * jax.nn.sigmoid(r)).astype(BF16))


def _in_value_body(x_ref, w_ref, g_ref, sc_ref, sh_ref, h_ref, o_ref, wb_ref):
    _weights_bf16(w_ref, wb_ref)
    h = (_rms(x_ref[...], g_ref[...]) * (1.0 + sc_ref[...]) + sh_ref[...]).astype(BF16)
    h_ref[...] = h
    for c0, r in _mm_chunks_t(h, wb_ref, 512):
        _heads_out(o_ref, c0, r.astype(BF16))


def _chunk_cumsum(g, c):
    rows, width = g.shape
    sub = V7X_SUBLANES
    x = g.reshape(rows // sub, sub, width)
    pos = lax.broadcasted_iota(jnp.int32, x.shape, 1)
    sh = 1
    while sh < sub:
        x = x + jnp.where(pos >= sh, pltpu.roll(x, sh, 1), 0.0)
        sh *= 2
    x = x.reshape(rows // c, c // sub, sub, width)
    run = None
    out = []
    for k in range(c // sub):
        blk = x[:, k]
        blk = blk if run is None else blk + run
        out.append(blk)
        run = blk[:, sub - 1:sub, :]
    return jnp.stack(out, axis=1).reshape(rows, width)


def _in_forget_body(h_ref, w_ref, lb_ref, b_ref, k_ref, wb_ref):
    _weights_bf16(w_ref, wb_ref)
    logits = lb_ref[...]
    e = jnp.exp(logits - jnp.max(logits, axis=0, keepdims=True))
    lb_all = e[0:1, :] / jnp.sum(e, axis=0, keepdims=True)
    for c0, r in _mm_chunks_t(h_ref[...], wb_ref, 512):
        lb = lb_all[:, c0:c0 + 512]
        f = lb + (1.0 - lb) * jax.nn.sigmoid(r)
        _heads_out(b_ref, c0, _chunk_cumsum(jnp.log2(f), HGRN_CHUNK))
        _heads_out(k_ref, c0, (1.0 - f).astype(BF16))


def _mla_in_body(h_ref, c_ref, a_ref, b_ref, w_ref, gq_ref, gkv_ref, wuq_ref, wukv_ref,
                 q_ref, k_ref, v_ref, wb_ref):
    _weights_bf16(w_ref, wb_ref)
    h = h_ref[...]
    c, a, b = c_ref[...], a_ref[...], b_ref[...]
    lat = dict(_mm_chunks_t(h, wb_ref, LORA))
    cq = _rms(lat[0], gq_ref[...]).astype(BF16)
    ckv = _rms(lat[LORA], gkv_ref[...]).astype(BF16)
    kr = _rope128(lat[2 * LORA], c, a, b).astype(BF16)

    scale = QK_DIM ** -0.5 * LOG2E
    for c0, r in _mm_chunks(cq, wuq_ref, 2 * QK_PAD):
        for hh in range(2):
            head = c0 // QK_PAD + hh
            nope = r[:, hh * QK_PAD:hh * QK_PAD + HEAD_DIM]
            rope = _rope128(r[:, hh * QK_PAD + HEAD_DIM:(hh + 1) * QK_PAD], c, a, b)
            q_ref[head, :, 0:HEAD_DIM] = (nope * scale).astype(BF16)
            q_ref[head, :, HEAD_DIM:QK_PAD] = (rope * scale).astype(BF16)
    for c0, r in _mm_chunks(ckv, wukv_ref, 2 * QK_PAD):
        for hh in range(2):
            head = c0 // QK_PAD + hh
            k_ref[head, :, 0:HEAD_DIM] = r[:, hh * QK_PAD:hh * QK_PAD + HEAD_DIM].astype(BF16)
            k_ref[head, :, HEAD_DIM:QK_PAD] = kr
            v_ref[head] = r[:, hh * QK_PAD + HEAD_DIM:(hh + 1) * QK_PAD].astype(BF16)


def _in_gate_body(h_ref, w_ref, o_ref, wb_ref):
    _weights_bf16(w_ref, wb_ref)
    for c0, r in _mm_chunks_t(h_ref[...], wb_ref, 512):
        o_ref[:, c0:c0 + 512] = jax.nn.sigmoid(r).astype(BF16)


def _attn_body(q_ref, k_ref, v_ref, o_ref, m_ref, acc_ref, p_ref, alpha_ref, *, tq, tk):
    qi = pl.program_id(1)
    nh = q_ref.shape[0]
    m_ref[...] = jnp.full(m_ref.shape, NEG, F32)
    acc_ref[...] = jnp.zeros(acc_ref.shape, F32)
    kv_rows = lambda j: pl.ds(pl.multiple_of(j * tk, tk), tk)

    def probs(j, slot, masked):
        for hh in range(nh):
            s = lax.dot_general(q_ref[hh], k_ref[hh, kv_rows(j), :], (((1,), (1,)), ((), ())),
                                preferred_element_type=F32)
            if masked:
                row = lax.broadcasted_iota(jnp.int32, s.shape, 0)
                col = lax.broadcasted_iota(jnp.int32, s.shape, 1)
                s = jnp.where(col - row <= qi * tq - j * tk, s, NEG)
            m_old = m_ref[hh]
            m_new = jnp.maximum(m_old, jnp.max(s, axis=-1, keepdims=True))
            m_ref[hh] = m_new
            alpha_ref[slot, hh] = jnp.exp2(m_old - m_new)
            p_ref[slot, hh] = jnp.exp2(s - m_new).astype(BF16)

    def accumulate(j, slot):
        ones = jnp.ones((tk, HEAD_DIM), BF16)
        for hh in range(nh):
            v_ext = jnp.concatenate([v_ref[hh, kv_rows(j), :], ones], axis=-1)
            acc_ref[hh] = (alpha_ref[slot, hh] * acc_ref[hh]
                           + jnp.dot(p_ref[slot, hh], v_ext, preferred_element_type=F32))

    def step(j, slot, masked):
        probs(j + 1, 1 - slot, masked)
        accumulate(j, slot)

    n_last = (qi * tq) // tk
    n_plain = jnp.maximum(n_last - 1, 0)

    @pl.when(n_last >= 1)
    def _():
        probs(0, 0, False)

    @pl.when(n_last == 0)
    def _():
        probs(0, 0, True)

    def pair(i, carry):
        step(2 * i, 0, False)
        step(2 * i + 1, 1, False)
        return carry

    lax.fori_loop(0, n_plain // 2, pair, 0)

    @pl.when(n_plain % 2 == 1)
    def _():
        step(n_plain - 1, 0, False)

    for parity in (0, 1):
        @pl.when((n_last >= 1) & ((n_last - 1) % 2 == parity))
        def _():
            step(n_last - 1, parity, True)

    for parity in (0, 1):
        @pl.when(n_last % 2 == parity)
        def _():
            accumulate(n_last, parity)
    for hh in range(nh):
        acc = acc_ref[hh]
        o_ref[hh] = (acc[:, :HEAD_DIM] / acc[:, HEAD_DIM:]).astype(BF16)


def _attention(q, k, v):
    h, s, _ = q.shape
    tq = min(ATTN_TILE, s)
    tk = min(ATTN_KV_TILE, s)
    nh = ATTN_HEADS
    assert tk % tq == 0
    kv_bytes = nh * (_nbytes((s, QK_PAD), BF16) + _nbytes((s, HEAD_DIM), BF16))
    scratch = [pltpu.VMEM((nh, tq, 1), F32), pltpu.VMEM((nh, tq, QK_PAD), F32),
               pltpu.VMEM((2, nh, tq, tk), BF16), pltpu.VMEM((2, nh, tq, 1), F32)]
    scratch_bytes = nh * (3 * _nbytes((tq, V7X_LANES), F32) + _nbytes((tq, QK_PAD), F32)
                          + 2 * _nbytes((tq, tk), BF16))
    return pl.pallas_call(
        functools.partial(_attn_body, tq=tq, tk=tk),
        grid=(h // nh, s // tq),
        in_specs=[pl.BlockSpec((nh, tq, QK_PAD), lambda hh, i: (hh, i, 0)),
                  pl.BlockSpec((nh, s, QK_PAD), lambda hh, i: (hh, 0, 0), pipeline_mode=pl.Buffered(1)),
                  pl.BlockSpec((nh, s, HEAD_DIM), lambda hh, i: (hh, 0, 0), pipeline_mode=pl.Buffered(1))],
        out_specs=pl.BlockSpec((nh, tq, HEAD_DIM), lambda hh, i: (hh, i, 0)),
        out_shape=_sds((h, s, HEAD_DIM), BF16),
        scratch_shapes=scratch,
        compiler_params=pltpu.CompilerParams(
            dimension_semantics=("arbitrary", "arbitrary"),
            vmem_limit_bytes=_vmem_limit(kv_bytes, 4 * nh * _nbytes((tq, QK_PAD), BF16), scratch_bytes,
                                         3 * nh * _nbytes((tq, tk), F32))),
        name="attn",
    )(q, k, v)


def _hgrn_chunk(qf, kf, vb, b, st, c):
    nt = (((1,), (1,)), ((), ()))

    o = lax.dot_general((qf * jnp.exp2(b)).astype(BF16), st.astype(BF16), nt, preferred_element_type=F32)

    row = lax.broadcasted_iota(jnp.int32, (c, c), 0)
    col = lax.broadcasted_iota(jnp.int32, (c, c), 1)
    shp8 = (c // V7X_SUBLANES, V7X_SUBLANES, HEAD_DIM)
    pos8 = lax.broadcasted_iota(jnp.int32, shp8, 1)
    scores = None
    m = c // 2
    while m >= 1:
        if m >= V7X_SUBLANES:
            shp = (c // (2 * m), 2, m, HEAD_DIM)
            b4, q4, k4 = b.reshape(shp), qf.reshape(shp), kf.reshape(shp)
            mid = b4[:, 0, m - 1:m, :]
            q_second = jnp.exp2(b4[:, 1] - mid) * q4[:, 1]
            k_first = jnp.exp2(mid - b4[:, 0]) * k4[:, 0]
            zero = jnp.zeros_like(q_second)
            qt = jnp.stack([zero, q_second], axis=1).reshape(c, HEAD_DIM)
            kt = jnp.stack([k_first, zero], axis=1).reshape(c, HEAD_DIM)
        elif m >= 2:
            b3 = b.reshape(shp8)
            if m == 4:
                mid = b3[:, 3:4, :]
                second = pos8 >= 4
            else:
                mid = jnp.where(pos8 < 4, b3[:, 1:2, :], b3[:, 5:6, :])
                second = (pos8 & 3) >= 2
            d = b3 - mid
            qt = (jnp.exp2(jnp.where(second, d, NEG)) * qf.reshape(shp8)).reshape(c, HEAD_DIM)
            kt = (jnp.exp2(jnp.where(second, NEG, -d)) * kf.reshape(shp8)).reshape(c, HEAD_DIM)
        else:
            odd = (lax.broadcasted_iota(jnp.int32, b.shape, 0) & 1) == 1
            qt = jnp.exp2(jnp.where(odd, b - pltpu.roll(b, 1, 0), NEG)) * qf
            kt = jnp.where(odd, 0.0, kf)
        s_l = lax.dot_general(qt.astype(BF16), kt.astype(BF16), nt, preferred_element_type=F32)
        if scores is None:
            scores = s_l
        else:
            shift = (2 * m).bit_length() - 1
            scores = jnp.where((row >> shift) == (col >> shift), s_l, scores)
        m //= 2
    scores = jnp.where(row == col, jnp.sum(qf * kf, axis=-1, keepdims=True), scores)
    o = o + jnp.dot(scores.astype(BF16), vb, preferred_element_type=F32)

    b_last = b[c - 1:c, :]
    kd = (kf * jnp.exp2(b_last - b)).astype(BF16)
    st_new = st * jnp.exp2(b_last) + lax.dot_general(vb, kd, (((0,), (0,)), ((), ())),
                                                    preferred_element_type=F32)
    return o, st_new


def _hgrn_body(q_ref, b_ref, k_ref, v_ref, og_ref, gain_ref, o_ref, st_ref, *, c, rows):
    @pl.when(pl.program_id(1) == 0)
    def _():
        st_ref[...] = jnp.zeros_like(st_ref)

    gain = gain_ref[...]

    def step(i, carry):
        sl = pl.ds(pl.multiple_of(i * c, c), c)
        for hh in range(q_ref.shape[0]):
            o, st_new = _hgrn_chunk(q_ref[hh, sl, :].astype(F32), k_ref[hh, sl, :].astype(F32),
                                    v_ref[hh, sl, :], b_ref[hh, sl, :], st_ref[hh], c)
            st_ref[hh] = st_new
            o_ref[hh, sl, :] = (_rms(o, gain) * og_ref[hh, sl, :].astype(F32)).astype(BF16)
        return carry

    lax.fori_loop(0, rows // c, step, 0)


def _hgrn(q, b, k, v, og, gain):
    h, s, d = q.shape
    rows = min(HGRN_ROWS, s)
    c = HGRN_CHUNK
    hb = HGRN_HEADS
    spec = pl.BlockSpec((hb, rows, d), lambda hh, i: (hh, i, 0))
    return pl.pallas_call(
        functools.partial(_hgrn_body, c=c, rows=rows),
        grid=(h // hb, s // rows),
        in_specs=[spec, spec, spec, spec, spec, pl.BlockSpec((1, d), lambda hh, i: (0, 0))],
        out_specs=spec,
        out_shape=_sds((h, s, d), BF16),
        scratch_shapes=[pltpu.VMEM((hb, d, d), F32)],
        compiler_params=pltpu.CompilerParams(
            dimension_semantics=("arbitrary", "arbitrary"),
            vmem_limit_bytes=_vmem_limit(2 * 5 * hb * _nbytes((rows, d), F32), 64 * hb * _nbytes((c, d), F32))),
        name="hgrn",
    )(q, b, k, v, og, gain)


def _cat_heads(ref):
    return jnp.concatenate([ref[hh] for hh in range(ref.shape[0])], axis=-1)


def _mix_body(oa_ref, ob_ref, sa_ref, sb_ref, x_ref,
              woa_ref, wob_ref, wout_ref, gpost_ref, gate_ref, gpre_ref, sc_ref, sh_ref,
              x1_ref, h2_ref):
    ya = jnp.dot(_cat_heads(oa_ref), woa_ref[...], preferred_element_type=F32)
    yb = jnp.dot(_cat_heads(ob_ref), wob_ref[...], preferred_element_type=F32)
    merged = sa_ref[...].astype(F32) * ya + sb_ref[...].astype(F32) * yb
    y = jnp.dot(merged.astype(BF16), wout_ref[...], preferred_element_type=F32)
    x1 = x_ref[...] + gate_ref[...] * _rms(y, gpost_ref[...])
    x1_ref[...] = x1
    h2_ref[...] = (_rms(x1, gpre_ref[...]) * (1.0 + sc_ref[...]) + sh_ref[...]).astype(BF16)


def _gateup_body(h_ref, wg_ref, wu_ref, o_ref):
    h = h_ref[...]
    gte = jnp.dot(h, wg_ref[...].astype(BF16), preferred_element_type=F32)
    up = jnp.dot(h, wu_ref[...].astype(BF16), preferred_element_type=F32)
    o_ref[...] = (gte * jax.nn.sigmoid(gte) * up).astype(BF16)


def _gateup(h2, w_gu, d_ff):
    s, d = h2.shape
    tm = min(1024, s)
    tn = 512
    nj = d_ff // tn
    return pl.pallas_call(
        _gateup_body,
        grid=(s // tm, nj),
        in_specs=[pl.BlockSpec((tm, d), lambda i, j: (i, 0)),
                  pl.BlockSpec((None, d, tn), lambda i, j: (0, 0, j)),
                  pl.BlockSpec((None, d, tn), lambda i, j: (0, 0, j + nj))],
        out_specs=pl.BlockSpec((tm, tn), lambda i, j: (i, j)),
        out_shape=_sds((s, d_ff), BF16),
        compiler_params=pltpu.CompilerParams(
            dimension_semantics=("arbitrary", "arbitrary"),
            vmem_limit_bytes=_vmem_limit(2 * _nbytes((tm, d), BF16), 4 * _nbytes((d, tn), F32),
                                         2 * _nbytes((d, tn), BF16),
                                         2 * _nbytes((tm, tn), BF16), 4 * _nbytes((tm, tn), F32))),
        name="gateup",
    )(h2, w_gu, w_gu)


def _down_body(a_ref, x_ref, w_ref, gpost_ref, gate_ref, o_ref):
    y = jnp.dot(a_ref[...], w_ref[...], preferred_element_type=F32)
    o_ref[...] = x_ref[...] + gate_ref[...] * _rms(y, gpost_ref[...])


def kernel(x, c, positions, w_ada, b_ada, g_pre_mix, w_in, lb_logits, g_hg_out, w_o_hg, g_q_lora, w_uq,
           g_kv_lora, w_ukv, w_o_mla, w_out, g_post_mix, g_pre_ffn, w_gate_up, w_down, g_post_ffn):
    bsz, s, d = x.shape
    assert bsz == 1 and d == N_HEADS * HEAD_DIM
    d_ff = w_down.shape[1]
    x2 = x.reshape(s, d)
    row = lambda v: v.reshape(1, -1)
    hm = lambda dt: _sds((N_HEADS, s, HEAD_DIM), dt)

    w_in_t = jnp.swapaxes(w_in, 1, 2)
    o_cq, o_kr = 4 * d, 4 * d + 2 * LORA
    o_ga = o_kr + ROPE_DIM
    o_gb = o_ga + d
    w_uq_p = jnp.pad(w_uq[0].reshape(LORA, N_HEADS, QK_DIM),
                     ((0, 0), (0, 0), (0, QK_PAD - QK_DIM))).reshape(LORA, N_HEADS * QK_PAD).astype(BF16)
    w_ukv_b = w_ukv[0].astype(BF16)
    inv_freq = ROPE_THETA ** (-jnp.arange(0, ROPE_DIM, 2, dtype=F32) / ROPE_DIM)
    inv_row = jnp.tile(inv_freq, V7X_LANES // ROPE_HALF).reshape(1, V7X_LANES)

    mod = _ada(c.reshape(d, 1), w_ada[0], row(b_ada[0]))
    sh_m, sc_m, ga_m, sh_f, sc_f, ga_f = (mod[:, i * d:(i + 1) * d] for i in range(6))

    tm = min(1024, s)
    tab = _sds((s, V7X_LANES), F32)
    rc, ra, rb = _row_call(_ropetab_body, [positions.reshape(s, 1)], [inv_row], [tab, tab, tab],
                           tm, 8 * _nbytes((tm, V7X_LANES), F32), "ropetab")

    tm = min(512, s)
    tmp = 6 * _nbytes((tm, 512), F32)
    tm2 = min(1024, s)
    tmp2 = 4 * _nbytes((tm2, 512), F32)
    wsc = lambda n: [((n, d), BF16)]
    wt = lambda r0, n: (w_in_t, n, r0)
    h, v_h = _row_call(_in_value_body, [x2], [wt(2 * d, d), row(g_pre_mix[0]), sc_m, sh_m],
                       [_sds((s, d), BF16), hm(BF16)], tm, tmp + 3 * _nbytes((tm, d), F32), "in_v", wsc(d))
    (q_h,) = _row_call(_in_silu_body, [h], [wt(0, d)], [hm(BF16)], tm2, tmp2, "in_q", wsc(d))
    b_h, k_h = _row_call(_in_forget_body, [h], [wt(d, d), lb_logits], [hm(F32), hm(BF16)], tm, tmp, "in_f",
                         wsc(d))
    (og_h,) = _row_call(_in_silu_body, [h], [wt(3 * d, d)], [hm(BF16)], tm2, tmp2, "in_og", wsc(d))
    n_lat = 2 * LORA + V7X_LANES
    q_m, k_m, v_m = _row_call(
        _mla_in_body, [h, rc, ra, rb],
        [wt(o_cq, n_lat), row(g_q_lora[0]), row(g_kv_lora[0]), w_uq_p, w_ukv_b],
        [_sds((N_HEADS, s, QK_PAD), BF16), _sds((N_HEADS, s, QK_PAD), BF16), hm(BF16)],
        tm, tmp, "mla_in", wsc(n_lat))
    (sig_a,) = _row_call(_in_gate_body, [h], [wt(o_ga, d)], [_sds((s, d), BF16)], tm2, tmp2, "in_ga", wsc(d))
    (sig_b,) = _row_call(_in_gate_body, [h], [wt(o_gb, d)], [_sds((s, d), BF16)], tm2, tmp2, "in_gb", wsc(d))

    o_a = _hgrn(q_h, b_h, k_h, v_h, og_h, row(g_hg_out[0]))

    o_b = _attention(q_m, k_m, v_m)

    tm = min(256, s)
    x1, h2 = _row_call(
        _mix_body, [o_a, o_b, sig_a, sig_b, x2],
        [w_o_hg[0].astype(BF16), w_o_mla[0].astype(BF16), w_out[0].astype(BF16),
         row(g_post_mix[0]), ga_m, row(g_pre_ffn[0]), sc_f, sh_f],
        [_sds((s, d), F32), _sds((s, d), BF16)], tm, 8 * _nbytes((tm, d), F32), "mix")

    act = _gateup(h2, w_gate_up, d_ff)
    (out,) = _row_call(_down_body, [act, x1], [w_down[0].astype(BF16), row(g_post_ffn[0]), ga_f],
                       [_sds((s, d), F32)], tm, 4 * _nbytes((tm, d), F32), "down")
    return out.reshape(bsz, s, d)
```

```python
import functools

import jax
import jax.numpy as jnp
from jax import lax
from jax.experimental import pallas as pl
from jax.experimental.pallas import tpu as pltpu

F32 = jnp.float32
BF16 = jnp.bfloat16

V7X_VMEM_BYTES = 64 * 1024 * 1024
V7X_LANES = 128
V7X_SUBLANES = 8

EPS = 1e-6
N_HEADS = 16
HEAD_DIM = 128
ROPE_DIM = 64
ROPE_HALF = ROPE_DIM // 2
QK_DIM = HEAD_DIM + ROPE_DIM
QK_PAD = 2 * HEAD_DIM
LORA = 512
ROPE_THETA = 10000.0
NEG = -1e30
LOG2E = 1.4426950408889634

HGRN_CHUNK = 128
HGRN_ROWS = 512
HGRN_HEADS = 16
ATTN_TILE = 512
ATTN_KV_TILE = 1024
ATTN_HEADS = 4


def _vmem_limit(*nbytes):
    need = int(sum(nbytes))
    return int(min(V7X_VMEM_BYTES - (4 << 20), need + (8 << 20)))


def _nbytes(shape, dtype):
    n = 1
    for s in shape:
        n *= s
    return n * jnp.dtype(dtype).itemsize


def _row_spec(arr_shape, tm):
    if len(arr_shape) == 2:
        return pl.BlockSpec((tm, arr_shape[1]), lambda i: (i, 0))
    return pl.BlockSpec((arr_shape[0], tm, arr_shape[2]), lambda i: (0, i, 0))


def _const_spec(arr_shape):
    nd = len(arr_shape)
    return pl.BlockSpec(tuple(arr_shape), lambda i: (0,) * nd)


def _row_call(body, row_ins, const_ins, outs, tm, temp_bytes, name, scratch=()):
    m = row_ins[0].shape[-2]
    assert m % tm == 0, (m, tm)
    in_specs = [_row_spec(a.shape, tm) for a in row_ins]
    const_bytes = 0
    const_args = []
    for cin in const_ins:
        if isinstance(cin, tuple):
            arr, height, r0 = cin
            in_specs.append(pl.BlockSpec((None, pl.Element(height), pl.Element(arr.shape[2])),
                                         lambda i, r0=r0: (0, r0, 0), pipeline_mode=pl.Buffered(1)))
            const_bytes += _nbytes((height, arr.shape[2]), arr.dtype)
        else:
            arr = cin
            in_specs.append(_const_spec(arr.shape))
            const_bytes += _nbytes(arr.shape, arr.dtype)
        const_args.append(arr)
    out_specs = [_row_spec(o.shape, tm) for o in outs]
    tile = lambda s: tuple(s[:-2]) + (tm, s[-1])
    need = (2 * sum(_nbytes(tile(a.shape), a.dtype) for a in row_ins)
            + 2 * sum(_nbytes(tile(o.shape), o.dtype) for o in outs)
            + const_bytes + sum(_nbytes(shp, dt) for shp, dt in scratch))
    res = pl.pallas_call(
        body,
        grid=(m // tm,),
        in_specs=in_specs,
        out_specs=out_specs,
        out_shape=outs,
        scratch_shapes=[pltpu.VMEM(shp, dt) for shp, dt in scratch],
        compiler_params=pltpu.CompilerParams(
            dimension_semantics=("arbitrary",),
            vmem_limit_bytes=_vmem_limit(need, temp_bytes)),
        name=name,
    )(*row_ins, *const_args)
    return res


def _weights_bf16(w_ref, wb_ref):
    @pl.when(pl.program_id(0) == 0)
    def _():
        step = min(256, w_ref.shape[0])
        for r0 in range(0, w_ref.shape[0], step):
            wb_ref[r0:r0 + step, :] = w_ref[r0:r0 + step, :].astype(BF16)


def _mm_chunks_t(a, wt_ref, nc):
    n = wt_ref.shape[0]
    for c0 in range(0, n, nc):
        yield c0, lax.dot_general(a, wt_ref[c0:min(c0 + nc, n), :], (((1,), (1,)), ((), ())),
                                  preferred_element_type=F32)


def _sds(shape, dtype):
    return jax.ShapeDtypeStruct(tuple(shape), dtype)


def _rms(y, gain):
    return y * lax.rsqrt(jnp.mean(y * y, axis=-1, keepdims=True) + EPS) * gain


def _rope128(x, c, a, b):
    return x * c + pltpu.roll(x, 96, 1) * a + pltpu.roll(x, 32, 1) * b


def _ada_body(c_ref, w_ref, b_ref, o_ref):
    cc = c_ref[...]
    sc = cc * jax.nn.sigmoid(cc)
    o_ref[...] = jnp.sum(sc * w_ref[...], axis=0, keepdims=True) + b_ref[...]


def _ada(c_col, w, b_row):
    d, n = w.shape
    tn = 1024
    return pl.pallas_call(
        _ada_body,
        grid=(n // tn,),
        in_specs=[pl.BlockSpec((d, 1), lambda j: (0, 0)),
                  pl.BlockSpec((d, tn), lambda j: (0, j)),
                  pl.BlockSpec((1, tn), lambda j: (0, j))],
        out_specs=pl.BlockSpec((1, tn), lambda j: (0, j)),
        out_shape=_sds((1, n), F32),
        compiler_params=pltpu.CompilerParams(
            dimension_semantics=("arbitrary",),
            vmem_limit_bytes=_vmem_limit(3 * _nbytes((d, tn), F32), _nbytes((d, V7X_LANES), F32))),
        name="ada",
    )(c_col, w, b_row)


def _ropetab_body(pos_ref, inv_ref, c_ref, a_ref, b_ref):
    ang = pos_ref[...].astype(F32) * inv_ref[...]
    cos = jnp.cos(ang)
    sin = jnp.sin(ang)
    lane = lax.broadcasted_iota(jnp.int32, ang.shape, 1)
    c_ref[...] = jnp.where(lane < ROPE_DIM, cos, 0.0)
    a_ref[...] = jnp.where(lane < ROPE_HALF, -sin, 0.0)
    b_ref[...] = jnp.where((lane >= ROPE_HALF) & (lane < ROPE_DIM), sin, 0.0)


def _heads_out(o_ref, c0, val):
    for hh in range(val.shape[1] // HEAD_DIM):
        o_ref[c0 // HEAD_DIM + hh] = val[:, hh * HEAD_DIM:(hh + 1) * HEAD_DIM]


def _mm_chunks(a, w_ref, nc):
    n = w_ref.shape[1]
    for c0 in range(0, n, nc):
        yield c0, jnp.dot(a, w_ref[:, c0:c0 + nc], preferred_element_type=F32)


def _in_silu_body(h_ref, w_ref, o_ref, wb_ref):
    _weights_bf16(w_ref, wb_ref)
    for c0, r in _mm_chunks_t(h_ref[...], wb_ref, 512):
        _heads_out(o_ref, c0, (r * jax.nn.sigmoid(r)).astype(BF16))


def _in_value_body(x_ref, w_ref, g_ref, sc_ref, sh_ref, h_ref, o_ref, wb_ref):
    _weights_bf16(w_ref, wb_ref)
    h = (_rms(x_ref[...], g_ref[...]) * (1.0 + sc_ref[...]) + sh_ref[...]).astype(BF16)
    h_ref[...] = h
    for c0, r in _mm_chunks_t(h, wb_ref, 512):
        _heads_out(o_ref, c0, r.astype(BF16))


def _chunk_cumsum(g, c):
    rows, width = g.shape
    sub = V7X_SUBLANES
    x = g.reshape(rows // sub, sub, width)
    pos = lax.broadcasted_iota(jnp.int32, x.shape, 1)
    sh = 1
    while sh < sub:
        x = x + jnp.where(pos >= sh, pltpu.roll(x, sh, 1), 0.0)
        sh *= 2
    x = x.reshape(rows // c, c // sub, sub, width)
    run = None
    out = []
    for k in range(c // sub):
        blk = x[:, k]
        blk = blk if run is None else blk + run
        out.append(blk)
        run = blk[:, sub - 1:sub, :]
    return jnp.stack(out, axis=1).reshape(rows, width)


def _in_forget_body(h_ref, w_ref, lb_ref, b_ref, k_ref, wb_ref):
    _weights_bf16(w_ref, wb_ref)
    logits = lb_ref[...]
    e = jnp.exp(logits - jnp.max(logits, axis=0, keepdims=True))
    lb_all = e[0:1, :] / jnp.sum(e, axis=0, keepdims=True)
    for c0, r in _mm_chunks_t(h_ref[...], wb_ref, 512):
        lb = lb_all[:, c0:c0 + 512]
        f = lb + (1.0 - lb) * jax.nn.sigmoid(r)
        _heads_out(b_ref, c0, _chunk_cumsum(jnp.log2(f), HGRN_CHUNK))
        _heads_out(k_ref, c0, (1.0 - f).astype(BF16))


def _mla_in_body(h_ref, c_ref, a_ref, b_ref, w_ref, gq_ref, gkv_ref, wuq_ref, wukv_ref,
                 q_ref, k_ref, v_ref, wb_ref):
    _weights_bf16(w_ref, wb_ref)
    h = h_ref[...]
    c, a, b = c_ref[...], a_ref[...], b_ref[...]
    lat = dict(_mm_chunks_t(h, wb_ref, LORA))
    cq = _rms(lat[0], gq_ref[...]).astype(BF16)
    ckv = _rms(lat[LORA], gkv_ref[...]).astype(BF16)
    kr = _rope128(lat[2 * LORA], c, a, b).astype(BF16)

    scale = QK_DIM ** -0.5 * LOG2E
    for c0, r in _mm_chunks(cq, wuq_ref, 2 * QK_PAD):
        for hh in range(2):
            head = c0 // QK_PAD + hh
            nope = r[:, hh * QK_PAD:hh * QK_PAD + HEAD_DIM]
            rope = _rope128(r[:, hh * QK_PAD + HEAD_DIM:(hh + 1) * QK_PAD], c, a, b)
            q_ref[head, :, 0:HEAD_DIM] = (nope * scale).astype(BF16)
            q_ref[head, :, HEAD_DIM:QK_PAD] = (rope * scale).astype(BF16)
    for c0, r in _mm_chunks(ckv, wukv_ref, 2 * QK_PAD):
        for hh in range(2):
            head = c0 // QK_PAD + hh
            k_ref[head, :, 0:HEAD_DIM] = r[:, hh * QK_PAD:hh * QK_PAD + HEAD_DIM].astype(BF16)
            k_ref[head, :, HEAD_DIM:QK_PAD] = kr
            v_ref[head] = r[:, hh * QK_PAD + HEAD_DIM:(hh + 1) * QK_PAD].astype(BF16)


def _in_gate_body(h_ref, w_ref, o_ref, wb_ref):
    _weights_bf16(w_ref, wb_ref)
    for c0, r in _mm_chunks_t(h_ref[...], wb_ref, 512):
        o_ref[:, c0:c0 + 512] = jax.nn.sigmoid(r).astype(BF16)


def _attn_body(q_ref, k_ref, v_ref, o_ref, m_ref, acc_ref, p_ref, alpha_ref, *, tq, tk):
    qi = pl.program_id(1)
    nh = q_ref.shape[0]
    m_ref[...] = jnp.full(m_ref.shape, NEG, F32)
    acc_ref[...] = jnp.zeros(acc_ref.shape, F32)
    kv_rows = lambda j: pl.ds(pl.multiple_of(j * tk, tk), tk)

    def probs(j, slot, masked):
        for hh in range(nh):
            s = lax.dot_general(q_ref[hh], k_ref[hh, kv_rows(j), :], (((1,), (1,)), ((), ())),
                                preferred_element_type=F32)
            if masked:
                row = lax.broadcasted_iota(jnp.int32, s.shape, 0)
                col = lax.broadcasted_iota(jnp.int32, s.shape, 1)
                s = jnp.where(col - row <= qi * tq - j * tk, s, NEG)
            m_old = m_ref[hh]
            m_new = jnp.maximum(m_old, jnp.max(s, axis=-1, keepdims=True))
            m_ref[hh] = m_new
            alpha_ref[slot, hh] = jnp.exp2(m_old - m_new)
            p_ref[slot, hh] = jnp.exp2(s - m_new).astype(BF16)

    def accumulate(j, slot):
        ones = jnp.ones((tk, HEAD_DIM), BF16)
        for hh in range(nh):
            v_ext = jnp.concatenate([v_ref[hh, kv_rows(j), :], ones], axis=-1)
            acc_ref[hh] = (alpha_ref[slot, hh] * acc_ref[hh]
                           + jnp.dot(p_ref[slot, hh], v_ext, preferred_element_type=F32))

    def step(j, slot, masked):
        probs(j + 1, 1 - slot, masked)
        accumulate(j, slot)

    n_last = (qi * tq) // tk
    n_plain = jnp.maximum(n_last - 1, 0)

    @pl.when(n_last >= 1)
    def _():
        probs(0, 0, False)

    @pl.when(n_last == 0)
    def _():
        probs(0, 0, True)

    def pair(i, carry):
        step(2 * i, 0, False)
        step(2 * i + 1, 1, False)
        return carry

    lax.fori_loop(0, n_plain // 2, pair, 0)

    @pl.when(n_plain % 2 == 1)
    def _():
        step(n_plain - 1, 0, False)

    for parity in (0, 1):
        @pl.when((n_last >= 1) & ((n_last - 1) % 2 == parity))
        def _():
            step(n_last - 1, parity, True)

    for parity in (0, 1):
        @pl.when(n_last % 2 == parity)
        def _():
            accumulate(n_last, parity)
    for hh in range(nh):
        acc = acc_ref[hh]
        o_ref[hh] = (acc[:, :HEAD_DIM] / acc[:, HEAD_DIM:]).astype(BF16)


def _attention(q, k, v):
    h, s, _ = q.shape
    tq = min(ATTN_TILE, s)
    tk = min(ATTN_KV_TILE, s)
    nh = ATTN_HEADS
    assert tk % tq == 0
    kv_bytes = nh * (_nbytes((s, QK_PAD), BF16) + _nbytes((s, HEAD_DIM), BF16))
    scratch = [pltpu.VMEM((nh, tq, 1), F32), pltpu.VMEM((nh, tq, QK_PAD), F32),
               pltpu.VMEM((2, nh, tq, tk), BF16), pltpu.VMEM((2, nh, tq, 1), F32)]
    scratch_bytes = nh * (3 * _nbytes((tq, V7X_LANES), F32) + _nbytes((tq, QK_PAD), F32)
                          + 2 * _nbytes((tq, tk), BF16))
    return pl.pallas_call(
        functools.partial(_attn_body, tq=tq, tk=tk),
        grid=(h // nh, s // tq),
        in_specs=[pl.BlockSpec((nh, tq, QK_PAD), lambda hh, i: (hh, i, 0)),
                  pl.BlockSpec((nh, s, QK_PAD), lambda hh, i: (hh, 0, 0), pipeline_mode=pl.Buffered(1)),
                  pl.BlockSpec((nh, s, HEAD_DIM), lambda hh, i: (hh, 0, 0), pipeline_mode=pl.Buffered(1))],
        out_specs=pl.BlockSpec((nh, tq, HEAD_DIM), lambda hh, i: (hh, i, 0)),
        out_shape=_sds((h, s, HEAD_DIM), BF16),
        scratch_shapes=scratch,
        compiler_params=pltpu.CompilerParams(
            dimension_semantics=("arbitrary", "arbitrary"),
            vmem_limit_bytes=_vmem_limit(kv_bytes, 4 * nh * _nbytes((tq, QK_PAD), BF16), scratch_bytes,
                                         3 * nh * _nbytes((tq, tk), F32))),
        name="attn",
    )(q, k, v)


def _hgrn_chunk(qf, kf, vb, b, st, c):
    nt = (((1,), (1,)), ((), ()))

    o = lax.dot_general((qf * jnp.exp2(b)).astype(BF16), st.astype(BF16), nt, preferred_element_type=F32)

    row = lax.broadcasted_iota(jnp.int32, (c, c), 0)
    col = lax.broadcasted_iota(jnp.int32, (c, c), 1)
    shp8 = (c // V7X_SUBLANES, V7X_SUBLANES, HEAD_DIM)
    pos8 = lax.broadcasted_iota(jnp.int32, shp8, 1)
    scores = None
    m = c // 2
    while m >= 1:
        if m >= V7X_SUBLANES:
            shp = (c // (2 * m), 2, m, HEAD_DIM)
            b4, q4, k4 = b.reshape(shp), qf.reshape(shp), kf.reshape(shp)
            mid = b4[:, 0, m - 1:m, :]
            q_second = jnp.exp2(b4[:, 1] - mid) * q4[:, 1]
            k_first = jnp.exp2(mid - b4[:, 0]) * k4[:, 0]
            zero = jnp.zeros_like(q_second)
            qt = jnp.stack([zero, q_second], axis=1).reshape(c, HEAD_DIM)
            kt = jnp.stack([k_first, zero], axis=1).reshape(c, HEAD_DIM)
        elif m >= 2:
            b3 = b.reshape(shp8)
            if m == 4:
                mid = b3[:, 3:4, :]
                second = pos8 >= 4
            else:
                mid = jnp.where(pos8 < 4, b3[:, 1:2, :], b3[:, 5:6, :])
                second = (pos8 & 3) >= 2
            d = b3 - mid
            qt = (jnp.exp2(jnp.where(second, d, NEG)) * qf.reshape(shp8)).reshape(c, HEAD_DIM)
            kt = (jnp.exp2(jnp.where(second, NEG, -d)) * kf.reshape(shp8)).reshape(c, HEAD_DIM)
        else:
            odd = (lax.broadcasted_iota(jnp.int32, b.shape, 0) & 1) == 1
            qt = jnp.exp2(jnp.where(odd, b - pltpu.roll(b, 1, 0), NEG)) * qf
            kt = jnp.where(odd, 0.0, kf)
        s_l = lax.dot_general(qt.astype(BF16), kt.astype(BF16), nt, preferred_element_type=F32)
        if scores is None:
            scores = s_l
        else:
            shift = (2 * m).bit_length() - 1
            scores = jnp.where((row >> shift) == (col >> shift), s_l, scores)
        m //= 2
    scores = jnp.where(row == col, jnp.sum(qf * kf, axis=-1, keepdims=True), scores)
    o = o + jnp.dot(scores.astype(BF16), vb, preferred_element_type=F32)

    b_last = b[c - 1:c, :]
    kd = (kf * jnp.exp2(b_last - b)).astype(BF16)
    st_new = st * jnp.exp2(b_last) + lax.dot_general(vb, kd, (((0,), (0,)), ((), ())),
                                                    preferred_element_type=F32)
    return o, st_new


def _hgrn_body(q_ref, b_ref, k_ref, v_ref, og_ref, gain_ref, o_ref, st_ref, *, c, rows):
    @pl.when(pl.program_id(1) == 0)
    def _():
        st_ref[...] = jnp.zeros_like(st_ref)

    gain = gain_ref[...]

    def step(i, carry):
        sl = pl.ds(pl.multiple_of(i * c, c), c)
        for hh in range(q_ref.shape[0]):
            o, st_new = _hgrn_chunk(q_ref[hh, sl, :].astype(F32), k_ref[hh, sl, :].astype(F32),
                                    v_ref[hh, sl, :], b_ref[hh, sl, :], st_ref[hh], c)
            st_ref[hh] = st_new
            o_ref[hh, sl, :] = (_rms(o, gain) * og_ref[hh, sl, :].astype(F32)).astype(BF16)
        return carry

    lax.fori_loop(0, rows // c, step, 0)


def _hgrn(q, b, k, v, og, gain):
    h, s, d = q.shape
    rows = min(HGRN_ROWS, s)
    c = HGRN_CHUNK
    hb = HGRN_HEADS
    spec = pl.BlockSpec((hb, rows, d), lambda hh, i: (hh, i, 0))
    return pl.pallas_call(
        functools.partial(_hgrn_body, c=c, rows=rows),
        grid=(h // hb, s // rows),
        in_specs=[spec, spec, spec, spec, spec, pl.BlockSpec((1, d), lambda hh, i: (0, 0))],
        out_specs=spec,
        out_shape=_sds((h, s, d), BF16),
        scratch_shapes=[pltpu.VMEM((hb, d, d), F32)],
        compiler_params=pltpu.CompilerParams(
            dimension_semantics=("arbitrary", "arbitrary"),
            vmem_limit_bytes=_vmem_limit(2 * 5 * hb * _nbytes((rows, d), F32), 64 * hb * _nbytes((c, d), F32))),
        name="hgrn",
    )(q, b, k, v, og, gain)


def _cat_heads(ref):
    return jnp.concatenate([ref[hh] for hh in range(ref.shape[0])], axis=-1)


def _mix_body(oa_ref, ob_ref, sa_ref, sb_ref, x_ref,
              woa_ref, wob_ref, wout_ref, gpost_ref, gate_ref, gpre_ref, sc_ref, sh_ref,
              x1_ref, h2_ref):
    ya = jnp.dot(_cat_heads(oa_ref), woa_ref[...], preferred_element_type=F32)
    yb = jnp.dot(_cat_heads(ob_ref), wob_ref[...], preferred_element_type=F32)
    merged = sa_ref[...].astype(F32) * ya + sb_ref[...].astype(F32) * yb
    y = jnp.dot(merged.astype(BF16), wout_ref[...], preferred_element_type=F32)
    x1 = x_ref[...] + gate_ref[...] * _rms(y, gpost_ref[...])
    x1_ref[...] = x1
    h2_ref[...] = (_rms(x1, gpre_ref[...]) * (1.0 + sc_ref[...]) + sh_ref[...]).astype(BF16)


def _gateup_body(h_ref, wg_ref, wu_ref, o_ref):
    h = h_ref[...]
    gte = jnp.dot(h, wg_ref[...].astype(BF16), preferred_element_type=F32)
    up = jnp.dot(h, wu_ref[...].astype(BF16), preferred_element_type=F32)
    o_ref[...] = (gte * jax.nn.sigmoid(gte) * up).astype(BF16)


def _gateup(h2, w_gu, d_ff):
    s, d = h2.shape
    tm = min(1024, s)
    tn = 512
    nj = d_ff // tn
    return pl.pallas_call(
        _gateup_body,
        grid=(s // tm, nj),
        in_specs=[pl.BlockSpec((tm, d), lambda i, j: (i, 0)),
                  pl.BlockSpec((None, d, tn), lambda i, j: (0, 0, j)),
                  pl.BlockSpec((None, d, tn), lambda i, j: (0, 0, j + nj))],
        out_specs=pl.BlockSpec((tm, tn), lambda i, j: (i, j)),
        out_shape=_sds((s, d_ff), BF16),
        compiler_params=pltpu.CompilerParams(
            dimension_semantics=("arbitrary", "arbitrary"),
            vmem_limit_bytes=_vmem_limit(2 * _nbytes((tm, d), BF16), 4 * _nbytes((d, tn), F32),
                                         2 * _nbytes((d, tn), BF16),
                                         2 * _nbytes((tm, tn), BF16), 4 * _nbytes((tm, tn), F32))),
        name="gateup",
    )(h2, w_gu, w_gu)


def _down_body(a_ref, x_ref, w_ref, gpost_ref, gate_ref, o_ref):
    y = jnp.dot(a_ref[...], w_ref[...], preferred_element_type=F32)
    o_ref[...] = x_ref[...] + gate_ref[...] * _rms(y, gpost_ref[...])


def kernel(x, c, positions, w_ada, b_ada, g_pre_mix, w_in, lb_logits, g_hg_out, w_o_hg, g_q_lora, w_uq,
           g_kv_lora, w_ukv, w_o_mla, w_out, g_post_mix, g_pre_ffn, w_gate_up, w_down, g_post_ffn):
    bsz, s, d = x.shape
    assert bsz == 1 and d == N_HEADS * HEAD_DIM
    d_ff = w_down.shape[1]
    x2 = x.reshape(s, d)
    row = lambda v: v.reshape(1, -1)
    hm = lambda dt: _sds((N_HEADS, s, HEAD_DIM), dt)

    w_in_t = jnp.swapaxes(w_in, 1, 2)
    o_cq, o_kr = 4 * d, 4 * d + 2 * LORA
    o_ga = o_kr + ROPE_DIM
    o_gb = o_ga + d
    w_uq_p = jnp.pad(w_uq[0].reshape(LORA, N_HEADS, QK_DIM),
                     ((0, 0), (0, 0), (0, QK_PAD - QK_DIM))).reshape(LORA, N_HEADS * QK_PAD).astype(BF16)
    w_ukv_b = w_ukv[0].astype(BF16)
    inv_freq = ROPE_THETA ** (-jnp.arange(0, ROPE_DIM, 2, dtype=F32) / ROPE_DIM)
    inv_row = jnp.tile(inv_freq, V7X_LANES // ROPE_HALF).reshape(1, V7X_LANES)

    mod = _ada(c.reshape(d, 1), w_ada[0], row(b_ada[0]))
    sh_m, sc_m, ga_m, sh_f, sc_f, ga_f = (mod[:, i * d:(i + 1) * d] for i in range(6))

    tm = min(1024, s)
    tab = _sds((s, V7X_LANES), F32)
    rc, ra, rb = _row_call(_ropetab_body, [positions.reshape(s, 1)], [inv_row], [tab, tab, tab],
                           tm, 8 * _nbytes((tm, V7X_LANES), F32), "ropetab")

    tm = min(512, s)
    tmp = 6 * _nbytes((tm, 512), F32)
    tm2 = min(1024, s)
    tmp2 = 4 * _nbytes((tm2, 512), F32)
    wsc = lambda n: [((n, d), BF16)]
    wt = lambda r0, n: (w_in_t, n, r0)
    h, v_h = _row_call(_in_value_body, [x2], [wt(2 * d, d), row(g_pre_mix[0]), sc_m, sh_m],
                       [_sds((s, d), BF16), hm(BF16)], tm, tmp + 3 * _nbytes((tm, d), F32), "in_v", wsc(d))
    (q_h,) = _row_call(_in_silu_body, [h], [wt(0, d)], [hm(BF16)], tm2, tmp2, "in_q", wsc(d))
    b_h, k_h = _row_call(_in_forget_body, [h], [wt(d, d), lb_logits], [hm(F32), hm(BF16)], tm, tmp, "in_f",
                         wsc(d))
    (og_h,) = _row_call(_in_silu_body, [h], [wt(3 * d, d)], [hm(BF16)], tm2, tmp2, "in_og", wsc(d))
    n_lat = 2 * LORA + V7X_LANES
    q_m, k_m, v_m = _row_call(
        _mla_in_body, [h, rc, ra, rb],
        [wt(o_cq, n_lat), row(g_q_lora[0]), row(g_kv_lora[0]), w_uq_p, w_ukv_b],
        [_sds((N_HEADS, s, QK_PAD), BF16), _sds((N_HEADS, s, QK_PAD), BF16), hm(BF16)],
        tm, tmp, "mla_in", wsc(n_lat))
    (sig_a,) = _row_call(_in_gate_body, [h], [wt(o_ga, d)], [_sds((s, d), BF16)], tm2, tmp2, "in_ga", wsc(d))
    (sig_b,) = _row_call(_in_gate_body, [h], [wt(o_gb, d)], [_sds((s, d), BF16)], tm2, tmp2, "in_gb", wsc(d))

    o_a = _hgrn(q_h, b_h, k_h, v_h, og_h, row(g_hg_out[0]))

    o_b = _attention(q_m, k_m, v_m)

    tm = min(256, s)
    x1, h2 = _row_call(
        _mix_body, [o_a, o_b, sig_a, sig_b, x2],
        [w_o_hg[0].astype(BF16), w_o_mla[0].astype(BF16), w_out[0].astype(BF16),
         row(g_post_mix[0]), ga_m, row(g_pre_ffn[0]), sc_f, sh_f],
        [_sds((s, d), F32), _sds((s, d), BF16)], tm, 8 * _nbytes((tm, d), F32), "mix")

    act = _gateup(h2, w_gate_up, d_ff)
    (out,) = _row_call(_down_body, [act, x1], [w_down[0].astype(BF16), row(g_post_ffn[0]), ga_f],
                       [_sds((s, d), F32)], tm, 4 * _nbytes((tm, d), F32), "down")
    return out.reshape(bsz, s, d)
```
